```python
import math
import jax, jax.numpy as jnp
from jax import lax
import numpy as np

D_MODEL = 1024
BATCH = 16
SEQ = 4096
DEPTH = 2

D_MIX = D_MODEL
W_CONV = D_MIX // 4
W_CFM = D_MIX // 4
W_SSM = D_MIX // 4
W_ATT = D_MIX - W_CONV - W_CFM - W_SSM
SHORT_K = 3
CFM_K = 31
SSM_CH = 16
SSM_GROUPS = W_SSM // SSM_CH
SSM_STATE = 64
HEAD_DIM = 64
N_HEADS = W_ATT // HEAD_DIM
DILATED_CFG = ((128, 1), (512, 4), (2048, 16))
ROPE_THETA = 10000.0
D_FF = -(-8 * D_MODEL // (3 * 256)) * 256
PROJ_SPLITS = (W_CONV, W_CONV, W_CONV, W_CFM, W_CFM, W_SSM, W_ATT, W_ATT, W_ATT)
N_IN = sum(PROJ_SPLITS)
GROUP_SPLITS = (W_CONV, W_CFM, W_SSM, W_ATT)
EPS = 1e-6

kernel_name = 'hybrid_parallel_group_decoder'


def _split_points(sizes):
    pts, acc = [], 0
    for s in sizes[:-1]:
        acc += s
        pts.append(acc)
    return pts


def rmsnorm(x, g):
    xf = x.astype(jnp.float32)
    y = xf * lax.rsqrt(jnp.mean(xf * xf, axis=-1, keepdims=True) + EPS)
    return (y * g.astype(jnp.float32)).astype(x.dtype)


def layernorm(x, g, b):
    xf = x.astype(jnp.float32)
    mu = jnp.mean(xf, axis=-1, keepdims=True)
    xc = xf - mu
    var = jnp.mean(xc * xc, axis=-1, keepdims=True)
    y = xc * lax.rsqrt(var + EPS) * g.astype(jnp.float32) + b.astype(jnp.float32)
    return y.astype(x.dtype)


def causal_depthwise_conv(x, w):
    k, c = w.shape
    return lax.conv_general_dilated(
        x, w[:, None, :].astype(x.dtype), window_strides=(1,),
        padding=((k - 1, 0),), dimension_numbers=('NWC', 'WIO', 'NWC'),
        feature_group_count=c)


def rope(x):
    seq, hd = x.shape[1], x.shape[-1]
    inv = ROPE_THETA ** (-jnp.arange(0, hd, 2, dtype=jnp.float32) / hd)
    ang = jnp.arange(seq, dtype=jnp.float32)[:, None] * inv[None, :]
    cos = jnp.cos(ang)[None, :, None, :]
    sin = jnp.sin(ang)[None, :, None, :]
    xf = x.astype(jnp.float32)
    x1, x2 = xf[..., :hd // 2], xf[..., hd // 2:]
    return jnp.concatenate([x1 * cos - x2 * sin, x2 * cos + x1 * sin], axis=-1).astype(x.dtype)


def short_conv_mixer(h, gate_b, gate_c, w):
    return gate_b * causal_depthwise_conv(gate_c * h, w)


def conformer_conv_mixer(a, g, dw_w, dw_b, ln_g, ln_b):
    z = a * jax.nn.sigmoid(g)
    z = causal_depthwise_conv(z, dw_w) + dw_b.astype(z.dtype)
    z = layernorm(z, ln_g, ln_b)
    return jax.nn.silu(z)


def s5_mixer(u, a_re, a_im, log_dt, b_re, b_im, c_re, c_im, d_skip, w_glu):
    bsz, seq, _ = u.shape
    f32 = jnp.float32
    uf = u.astype(f32).reshape(bsz, seq, SSM_GROUPS, SSM_CH)
    dt = jnp.exp(log_dt.astype(f32))[:, None]
    ar, ai = a_re.astype(f32), a_im.astype(f32)
    mag = jnp.exp(ar * dt)
    lr, li = mag * jnp.cos(ai * dt), mag * jnp.sin(ai * dt)
    den = ar * ar + ai * ai
    nr, ni = lr - 1.0, li
    fr = (nr * ar + ni * ai) / den
    fi = (ni * ar - nr * ai) / den
    br, bi = b_re.astype(f32), b_im.astype(f32)
    bbar_re = fr[..., None] * br - fi[..., None] * bi
    bbar_im = fr[..., None] * bi + fi[..., None] * br
    xr = jnp.einsum('blgh,gph->blgp', uf, bbar_re)
    xi = jnp.einsum('blgh,gph->blgp', uf, bbar_im)
    a_full_r = jnp.broadcast_to(lr, xr.shape)
    a_full_i = jnp.broadcast_to(li, xr.shape)

    def combine(e1, e2):
        a1r, a1i, b1r, b1i = e1
        a2r, a2i, b2r, b2i = e2
        return (a2r * a1r - a2i * a1i, a2r * a1i + a2i * a1r,
                a2r * b1r - a2i * b1i + b2r, a2r * b1i + a2i * b1r + b2i)

    _, _, sr, si = lax.associative_scan(combine, (a_full_r, a_full_i, xr, xi), axis=1)
    y = (jnp.einsum('blgp,ghp->blgh', sr, c_re.astype(f32))
         - jnp.einsum('blgp,ghp->blgh', si, c_im.astype(f32)))
    y = y.reshape(bsz, seq, W_SSM) + d_skip.astype(f32) * uf.reshape(bsz, seq, W_SSM)
    z = jax.nn.gelu(y)
    gl = z @ w_glu.astype(f32)
    out = gl[..., :W_SSM] * jax.nn.sigmoid(gl[..., W_SSM:])
    return out.astype(u.dtype)


def dilated_window_attention(q, k, v, window, dilation):
    bsz, seq, nh, hd = q.shape
    n_keys = window // dilation
    blk = n_keys
    n = seq // dilation
    nb = -(-n // blk)
    pad = nb * blk - n

    def regroup(t):
        t = t.reshape(bsz, n, dilation, nh, hd).transpose(0, 2, 3, 1, 4)
        t = jnp.pad(t, ((0, 0), (0, 0), (0, 0), (0, pad), (0, 0)))
        return t.reshape(bsz, dilation, nh, nb, blk, hd)

    qb, kb, vb = regroup(q), regroup(k), regroup(v)

    def prev(t):
        return jnp.pad(t, ((0, 0), (0, 0), (0, 0), (1, 0), (0, 0), (0, 0)))[:, :, :, :-1]

    kk = jnp.concatenate([prev(kb), kb], axis=4)
    vv = jnp.concatenate([prev(vb), vb], axis=4)
    s = jnp.einsum('bdhnqe,bdhnke->bdhnqk', qb, kk,
                   preferred_element_type=jnp.float32) * (hd ** -0.5)
    bi = jnp.arange(nb)[:, None, None]
    qi = jnp.arange(blk)[None, :, None]
    kj = jnp.arange(2 * blk)[None, None, :]
    rel = blk + qi - kj
    valid = (rel >= 0) & (rel <= n_keys) & ((bi > 0) | (kj >= blk))
    s = jnp.where(valid, s, -jnp.inf)
    m = jnp.max(s, axis=-1, keepdims=True)
    p = jnp.exp(s - m)
    den = jnp.sum(p, axis=-1, keepdims=True)
    o = jnp.einsum('bdhnqk,bdhnke->bdhnqe', p, vv.astype(jnp.float32)) / den
    lse = (m + jnp.log(den))[..., 0]
    o = o.reshape(bsz, dilation, nh, nb * blk, hd)[:, :, :, :n]
    o = o.transpose(0, 3, 1, 2, 4).reshape(bsz, seq, nh, hd)
    lse = lse.reshape(bsz, dilation, nh, nb * blk)[:, :, :, :n]
    lse = lse.transpose(0, 3, 1, 2).reshape(bsz, seq, nh)
    return o, lse


def dilated_attention_mixer(q, k, v):
    bsz, seq, _ = q.shape
    q = rope(q.reshape(bsz, seq, N_HEADS, HEAD_DIM))
    k = rope(k.reshape(bsz, seq, N_HEADS, HEAD_DIM))
    v = v.reshape(bsz, seq, N_HEADS, HEAD_DIM)
    outs, lses = [], []
    for window, dilation in DILATED_CFG:
        o, lse = dilated_window_attention(q, k, v, window, dilation)
        outs.append(o)
        lses.append(lse)
    wts = jax.nn.softmax(jnp.stack(lses, axis=0), axis=0)
    o = jnp.sum(wts[..., None] * jnp.stack(outs, axis=0), axis=0)
    return o.reshape(bsz, seq, W_ATT).astype(q.dtype)


def setup_inputs(seed: int = 0) -> dict:
    key = jax.random.key(seed)
    ks = iter(jax.random.split(key, 32))
    f32 = jnp.float32

    def nrm(shape, scale):
        return jax.random.normal(next(ks), shape, f32) * scale

    x = nrm((BATCH, SEQ, D_MODEL), 1.0)
    norm_mix_g = 1.0 + nrm((DEPTH, D_MODEL), 0.02)
    w_in = nrm((DEPTH, D_MODEL, N_IN), D_MODEL ** -0.5)
    conv3_w = nrm((DEPTH, SHORT_K, W_CONV), SHORT_K ** -0.5)
    cfm_dw_w = nrm((DEPTH, CFM_K, W_CFM), CFM_K ** -0.5)
    cfm_dw_b = nrm((DEPTH, W_CFM), 0.02)
    cfm_ln_g = 1.0 + nrm((DEPTH, W_CFM), 0.02)
    cfm_ln_b = nrm((DEPTH, W_CFM), 0.02)
    n_idx = jnp.arange(SSM_STATE, dtype=f32)
    s5_a_re = -0.5 + nrm((DEPTH, SSM_GROUPS, SSM_STATE), 0.01)
    s5_a_im = math.pi * n_idx + nrm((DEPTH, SSM_GROUPS, SSM_STATE), 0.01)
    s5_log_dt = jax.random.uniform(next(ks), (DEPTH, SSM_GROUPS), f32,
                                   math.log(1e-3), math.log(1e-1))
    s5_b_re = nrm((DEPTH, SSM_GROUPS, SSM_STATE, SSM_CH), (2 * SSM_CH) ** -0.5)
    s5_b_im = nrm((DEPTH, SSM_GROUPS, SSM_STATE, SSM_CH), (2 * SSM_CH) ** -0.5)
    s5_c_re = nrm((DEPTH, SSM_GROUPS, SSM_CH, SSM_STATE), SSM_STATE ** -0.5)
    s5_c_im = nrm((DEPTH, SSM_GROUPS, SSM_CH, SSM_STATE), SSM_STATE ** -0.5)
    s5_d = nrm((DEPTH, W_SSM), 0.1)
    s5_glu_w = nrm((DEPTH, W_SSM, 2 * W_SSM), W_SSM ** -0.5)
    grp_norm_g = 1.0 + nrm((DEPTH, D_MIX), 0.02)
    w_out = nrm((DEPTH, D_MIX, D_MODEL), D_MIX ** -0.5)
    norm_ffn_g = 1.0 + nrm((DEPTH, D_MODEL), 0.02)
    w_gate = nrm((DEPTH, D_MODEL, D_FF), D_MODEL ** -0.5)
    w_up = nrm((DEPTH, D_MODEL, D_FF), D_MODEL ** -0.5)
    w_down = nrm((DEPTH, D_FF, D_MODEL), D_FF ** -0.5)
    final_norm_g = 1.0 + nrm((D_MODEL,), 0.02)
    return {'x': x, 'norm_mix_g': norm_mix_g, 'w_in': w_in, 'conv3_w': conv3_w,
            'cfm_dw_w': cfm_dw_w, 'cfm_dw_b': cfm_dw_b, 'cfm_ln_g': cfm_ln_g,
            'cfm_ln_b': cfm_ln_b, 's5_a_re': s5_a_re, 's5_a_im': s5_a_im,
            's5_log_dt': s5_log_dt, 's5_b_re': s5_b_re, 's5_b_im': s5_b_im,
            's5_c_re': s5_c_re, 's5_c_im': s5_c_im, 's5_d': s5_d, 's5_glu_w': s5_glu_w,
            'grp_norm_g': grp_norm_g, 'w_out': w_out, 'norm_ffn_g': norm_ffn_g,
            'w_gate': w_gate, 'w_up': w_up, 'w_down': w_down, 'final_norm_g': final_norm_g}


def reference(x, norm_mix_g, w_in, conv3_w, cfm_dw_w, cfm_dw_b, cfm_ln_g, cfm_ln_b,
              s5_a_re, s5_a_im, s5_log_dt, s5_b_re, s5_b_im, s5_c_re, s5_c_im, s5_d,
              s5_glu_w, grp_norm_g, w_out, norm_ffn_g, w_gate, w_up, w_down,
              final_norm_g):
    proj_pts = _split_points(PROJ_SPLITS)
    grp_pts = _split_points(GROUP_SPLITS)
    for layer in range(DEPTH):
        h = rmsnorm(x, norm_mix_g[layer])
        p = h @ w_in[layer]
        (c_h, c_b, c_c, f_a, f_g, s_u, a_q, a_k, a_v) = jnp.split(p, proj_pts, axis=-1)
        y_conv = short_conv_mixer(c_h, c_b, c_c, conv3_w[layer])
        y_cfm = conformer_conv_mixer(f_a, f_g, cfm_dw_w[layer], cfm_dw_b[layer],
                                     cfm_ln_g[layer], cfm_ln_b[layer])
        y_ssm = s5_mixer(s_u, s5_a_re[layer], s5_a_im[layer], s5_log_dt[layer],
                         s5_b_re[layer], s5_b_im[layer], s5_c_re[layer], s5_c_im[layer],
                         s5_d[layer], s5_glu_w[layer])
        y_att = dilated_attention_mixer(a_q, a_k, a_v)
        g_conv, g_cfm, g_ssm, g_att = jnp.split(grp_norm_g[layer], grp_pts)
        mixed = jnp.concatenate([rmsnorm(y_conv, g_conv), rmsnorm(y_cfm, g_cfm),
                                 rmsnorm(y_ssm, g_ssm), rmsnorm(y_att, g_att)], axis=-1)
        x = x + (mixed @ w_out[layer]).astype(x.dtype)
        h = rmsnorm(x, norm_ffn_g[layer])
        ff = jax.nn.silu(h @ w_gate[layer]) * (h @ w_up[layer])
        x = x + (ff @ w_down[layer]).astype(x.dtype)
    return rmsnorm(x, final_norm_g)
```

```python
import functools
import math

import jax
import jax.numpy as jnp
from jax import lax
from jax.experimental import pallas as pl
from jax.experimental.pallas import tpu as pltpu

F32 = jnp.float32
BF16 = jnp.bfloat16

EPS = 1e-6
HEAD_DIM = 64
SHORT_K = 3
CFM_K = 31
SSM_CH = 16
SSM_STATE = 64
SSM_CHUNK = 8
DILATED_CFG = ((128, 1), (512, 4), (2048, 16))
ROPE_THETA = 10000.0
ATT_BLK = 128
NEG_BIG = -1e30
VMEM_LIMIT = 56 * 1024 * 1024


def _cparams(*sem):
    return pltpu.CompilerParams(dimension_semantics=sem, vmem_limit_bytes=VMEM_LIMIT)


def _rms(x, g):
    return x * lax.rsqrt(jnp.mean(x * x, axis=-1, keepdims=True) + EPS) * g


def _resident(shape):
    nd = len(shape)
    return pl.BlockSpec(shape, lambda *_: (0,) * nd, pipeline_mode=pl.Buffered(1))


def _in_proj_body(x_ref, g_ref, w_ref, cq_ref, sq_ref, ck_ref, sk_ref,
                  pc_ref, pf_ref, u_ref, q_ref, k_ref, v_ref, *, wmix):
    h = _rms(x_ref[...], g_ref[...]).astype(BF16)

    def proj(lo, width):
        return jnp.dot(h, w_ref[:, lo:lo + width], preferred_element_type=F32)

    pc_ref[...] = proj(0, 3 * wmix).astype(BF16)
    pf_ref[...] = proj(3 * wmix, 2 * wmix).astype(BF16)
    u_ref[...] = proj(5 * wmix, wmix).astype(BF16)

    lane = lax.broadcasted_iota(jnp.int32, (1, wmix), 1)
    first_half = (lane % HEAD_DIM) < (HEAD_DIM // 2)

    def rope(t, c_ref, s_ref):
        partner = jnp.where(first_half, pltpu.roll(t, wmix - HEAD_DIM // 2, 1),
                            pltpu.roll(t, HEAD_DIM // 2, 1))
        return t * c_ref[...] + partner * s_ref[...]

    q_ref[...] = rope(proj(6 * wmix, wmix), cq_ref, sq_ref).astype(BF16)
    k_ref[...] = rope(proj(7 * wmix, wmix), ck_ref, sk_ref).astype(BF16)
    v_ref[...] = proj(8 * wmix, wmix).astype(BF16)


def _in_proj(x, g, w, rope_tabs, seq, tm):
    n, d = x.shape
    nin = w.shape[1]
    wmix = nin // 9
    per_seq = seq // tm
    row = lambda i: (i, 0)
    tab = lambda i: (i % per_seq, 0)
    outs = [(n, 3 * wmix), (n, 2 * wmix)] + [(n, wmix)] * 4
    return pl.pallas_call(
        functools.partial(_in_proj_body, wmix=wmix),
        grid=(n // tm,),
        in_specs=[pl.BlockSpec((tm, d), row), _resident((1, d)), _resident((d, nin))]
        + [pl.BlockSpec((tm, wmix), tab)] * 4,
        out_specs=[pl.BlockSpec((tm, s[1]), row) for s in outs],
        out_shape=[jax.ShapeDtypeStruct(s, BF16) for s in outs],
        compiler_params=_cparams("parallel"),
        name="in_proj",
    )(x, g.reshape(1, d), w, *rope_tabs)


CONV_ROWS = 32
CFM_HALO = 32
SHORT_HALO = 16


def _conv_body(pc_ref, pch_ref, pf_ref, pfh_ref, w3_ref, wdw_ref, bdw_ref, lng_ref, lnb_ref,
               gn_ref, o_ref, zb3, zs, *, wmix):
    t_rows = pc_ref.shape[0]
    keep = (pl.program_id(1) > 0).astype(F32)

    pc = pc_ref[...].astype(F32)
    ch = pc[:, 2 * wmix:] * pc[:, :wmix]
    hal = pch_ref[...].astype(F32)
    zb3[0:SHORT_HALO, :] = hal[:, 2 * wmix:] * hal[:, :wmix] * keep
    zb3[SHORT_HALO:, :] = ch
    conv = w3_ref[SHORT_K - 1:SHORT_K, :] * ch
    for k in range(SHORT_K - 1):
        off = SHORT_HALO - (SHORT_K - 1) + k
        conv = conv + w3_ref[k:k + 1, :] * zb3[off:off + t_rows, :]
    y_conv = pc[:, wmix:2 * wmix] * conv
    o_ref[:, :wmix] = _rms(y_conv, gn_ref[:, :wmix]).astype(BF16)

    pf = pf_ref[...].astype(F32)
    ph = pfh_ref[...].astype(F32)
    zs[0, 0:CFM_HALO, :] = ph[:, :wmix] * jax.nn.sigmoid(ph[:, wmix:]) * keep
    zs[0, CFM_HALO:, :] = pf[:, :wmix] * jax.nn.sigmoid(pf[:, wmix:])
    span = t_rows + CFM_HALO - 8
    for b in range(1, 8):
        zs[b, 0:span, :] = zs[0, b:b + span, :]

    base = CFM_HALO - (CFM_K - 1)

    def rows(c, carry):
        r0 = pl.multiple_of(c * CONV_ROWS, CONV_ROWS)
        acc = jnp.zeros((CONV_ROWS, wmix), F32)
        for k in range(CFM_K):
            a, b = divmod(base + k, 8)
            acc = acc + wdw_ref[k:k + 1, :] * zs[b, pl.ds(r0 + 8 * a, CONV_ROWS), :]
        acc = acc + bdw_ref[...]
        mu = jnp.mean(acc, axis=-1, keepdims=True)
        xc = acc - mu
        var = jnp.mean(xc * xc, axis=-1, keepdims=True)
        y = xc * lax.rsqrt(var + EPS) * lng_ref[...] + lnb_ref[...]
        y = y * jax.nn.sigmoid(y)
        o_ref[pl.ds(r0, CONV_ROWS), wmix:] = _rms(y, gn_ref[:, wmix:]).astype(BF16)
        return carry

    lax.fori_loop(0, t_rows // CONV_ROWS, rows, 0)


def _conv_mixers(pc, pf, w3, wdw, bdw, lng, lnb, gn, batch, seq, tt):
    n = pc.shape[0]
    wmix = pf.shape[1] // 2
    per_seq = seq // tt
    row = lambda b, i: (b * per_seq + i, 0)

    def halo(rows_):
        return lambda b, i: (jnp.maximum((b * per_seq + i) * (tt // rows_) - 1, 0), 0)

    vec = lambda a: a.reshape(1, -1)
    return pl.pallas_call(
        functools.partial(_conv_body, wmix=wmix),
        grid=(batch, per_seq),
        in_specs=[pl.BlockSpec((tt, 3 * wmix), row),
                  pl.BlockSpec((SHORT_HALO, 3 * wmix), halo(SHORT_HALO)),
                  pl.BlockSpec((tt, 2 * wmix), row),
                  pl.BlockSpec((CFM_HALO, 2 * wmix), halo(CFM_HALO)),
                  _resident((SHORT_K, wmix)), _resident((CFM_K, wmix)),
                  _resident((1, wmix)), _resident((1, wmix)), _resident((1, wmix)),
                  _resident((1, 2 * wmix))],
        out_specs=pl.BlockSpec((tt, 2 * wmix), row),
        out_shape=jax.ShapeDtypeStruct((n, 2 * wmix), BF16),
        scratch_shapes=[pltpu.VMEM((tt + SHORT_HALO, wmix), F32),
                        pltpu.VMEM((8, tt + CFM_HALO, wmix), F32)],
        compiler_params=_cparams("parallel", "arbitrary"),
        name="conv_mixers",
    )(pc, pc, pf, pf, w3, wdw, vec(bdw), vec(lng), vec(lnb), vec(gn))


def _cmul(ar, ai, br, bi):
    return ar * br - ai * bi, ar * bi + ai * br


def _ssm_mats(a_re, a_im, log_dt, b_re, b_im, c_re, c_im, n_steps):
    hi = lax.Precision.HIGHEST
    ng, ns = a_re.shape
    nh = b_re.shape[-1]
    tc = SSM_CHUNK
    dt = jnp.exp(log_dt)[:, None]
    mag = jnp.exp(a_re * dt)
    lr, li = mag * jnp.cos(a_im * dt), mag * jnp.sin(a_im * dt)
    den = a_re * a_re + a_im * a_im
    nr, ni = lr - 1.0, li
    fr = (nr * a_re + ni * a_im) / den
    fi = (ni * a_re - nr * a_im) / den
    bbr = fr[..., None] * b_re - fi[..., None] * b_im
    bbi = fr[..., None] * b_im + fi[..., None] * b_re

    pr, pi = [jnp.ones_like(lr)], [jnp.zeros_like(li)]
    for _ in range(tc):
        r, i = _cmul(pr[-1], pi[-1], lr, li)
        pr.append(r)
        pi.append(i)
    pwr, pwi = jnp.stack(pr), jnp.stack(pi)

    lbr, lbi = _cmul(pwr[:tc, :, :, None], pwi[:tc, :, :, None], bbr[None], bbi[None])
    kd = (jnp.einsum('ghp,dgpk->dghk', c_re, lbr, precision=hi)
          - jnp.einsum('ghp,dgpk->dghk', c_im, lbi, precision=hi))
    tau = jnp.arange(tc)[:, None]
    tt = jnp.arange(tc)[None, :]
    lag = tt - tau
    kfull = jnp.where((lag >= 0)[:, :, None, None, None], kd[jnp.clip(lag, 0, tc - 1)], 0.0)
    m_full = kfull.transpose(0, 2, 4, 1, 3)
    w_r = lbr[::-1].transpose(0, 1, 3, 2)
    w_i = lbi[::-1].transpose(0, 1, 3, 2)
    clr, cli = _cmul(c_re[None], c_im[None], pwr[1:, :, None, :], pwi[1:, :, None, :])
    g_r = clr.transpose(1, 3, 0, 2)
    g_i = -cli.transpose(1, 3, 0, 2)

    gh = ng // 2
    eye = jnp.eye(gh, dtype=F32)
    ms, ws, gs, aps = [], [], [], []
    for j in range(2):
        sl = slice(j * gh, (j + 1) * gh)
        mj = m_full[:, sl][:, :, :, :, None, :] * eye[None, :, None, None, :, None]
        ms.append(mj.reshape(tc * gh * nh, tc * gh * nh))
        wj = jnp.stack([w_r[:, sl], w_i[:, sl]], axis=3)
        wj = wj[:, :, :, :, None, :] * eye[None, :, None, None, :, None]
        ws.append(wj.reshape(tc * gh * nh, 2 * gh * ns))
        gj = jnp.stack([g_r[sl], g_i[sl]], axis=0)
        gj = gj[:, :, :, :, None, :] * eye[None, :, None, None, :, None]
        gs.append(gj.reshape(2 * gh * ns, tc * gh * nh))
        ar, ai = pwr[tc, sl].reshape(-1), pwi[tc, sl].reshape(-1)
        steps = []
        for _ in range(n_steps):
            steps.append(jnp.concatenate([ar, ai]))
            ar, ai = _cmul(ar, ai, ar, ai)
        aps.append(jnp.stack(steps))
    return (jnp.stack(ms).astype(BF16), jnp.stack(ws).astype(BF16), jnp.stack(gs).astype(BF16),
            jnp.stack(aps))


def _ssm_body(u_ref, m_ref, w_ref, g_ref, ap_ref, d_ref, o_ref, bre, bim, *, n_steps, half):
    rows_ = u_ref.shape[0]
    pad = bre.shape[0] - rows_
    ch = 2 * half
    nst = ap_ref.shape[2] // 2
    bre[0:pad, :] = jnp.zeros((pad, nst), F32)
    bim[0:pad, :] = jnp.zeros((pad, nst), F32)
    for j in range(2):
        lanes = [slice(t * ch + j * half, t * ch + (j + 1) * half) for t in range(SSM_CHUNK)]
        lhs = jnp.concatenate([u_ref[:, s] for s in lanes], axis=1)
        z = jnp.dot(lhs, w_ref[j], preferred_element_type=F32)
        cr, ci = z[:, :nst], z[:, nst:]
        for k in range(n_steps):
            s = 1 << k
            bre[pad:, :] = cr
            bim[pad:, :] = ci
            sr = bre[pad - s:pad - s + rows_, :]
            si = bim[pad - s:pad - s + rows_, :]
            ar = ap_ref[j, k:k + 1, :nst]
            ai = ap_ref[j, k:k + 1, nst:]
            cr = cr + ar * sr - ai * si
            ci = ci + ar * si + ai * sr
        bre[pad:, :] = cr
        bim[pad:, :] = ci
        s_in = jnp.concatenate([bre[pad - 1:pad - 1 + rows_, :], bim[pad - 1:pad - 1 + rows_, :]],
                               axis=1).astype(BF16)
        y = (jnp.dot(lhs, m_ref[j], preferred_element_type=F32)
             + jnp.dot(s_in, g_ref[j], preferred_element_type=F32))
        for t, s in enumerate(lanes):
            yy = y[:, t * half:(t + 1) * half] + d_ref[:, s] * u_ref[:, s].astype(F32)
            o_ref[:, s] = jax.nn.gelu(yy).astype(BF16)


def _ssm_core(u, mats, d_skip, batch, seq):
    n, ch = u.shape
    m, w, g, ap = mats
    rows_ = seq // SSM_CHUNK
    width = SSM_CHUNK * ch
    n_steps = ap.shape[1]
    pad = max(rows_ // 2, 8)
    u2 = u.reshape(n // SSM_CHUNK, width)
    d_row = jnp.tile(d_skip, SSM_CHUNK).reshape(1, width)
    nst = ap.shape[2] // 2
    out = pl.pallas_call(
        functools.partial(_ssm_body, n_steps=n_steps, half=ch // 2),
        grid=(batch,),
        in_specs=[pl.BlockSpec((rows_, width), lambda b: (b, 0)),
                  _resident(m.shape), _resident(w.shape), _resident(g.shape),
                  _resident(ap.shape), _resident((1, width))],
        out_specs=pl.BlockSpec((rows_, width), lambda b: (b, 0)),
        out_shape=jax.ShapeDtypeStruct(u2.shape, BF16),
        scratch_shapes=[pltpu.VMEM((pad + rows_, nst), F32), pltpu.VMEM((pad + rows_, nst), F32)],
        compiler_params=_cparams("parallel"),
        name="ssm_core",
    )(u2, m, w, g, ap, d_row)
    return out.reshape(n, ch)


def _ssm_glu_body(z_ref, w_ref, gn_ref, o_ref):
    wd = z_ref.shape[1]
    gl = jnp.dot(z_ref[...], w_ref[...], preferred_element_type=F32)
    y = gl[:, :wd] * jax.nn.sigmoid(gl[:, wd:])
    o_ref[...] = _rms(y, gn_ref[...]).astype(BF16)


def _ssm_glu(z, w, gn, tm):
    n, wd = z.shape
    return pl.pallas_call(
        _ssm_glu_body,
        grid=(n // tm,),
        in_specs=[pl.BlockSpec((tm, wd), lambda i: (i, 0)), _resident(w.shape), _resident((1, wd))],
        out_specs=pl.BlockSpec((tm, wd), lambda i: (i, 0)),
        out_shape=jax.ShapeDtypeStruct((n, wd), BF16),
        compiler_params=_cparams("parallel"),
        name="ssm_glu",
    )(z, w, gn.reshape(1, wd))


def _attn_body(q_ref, k_ref, kh_ref, v_ref, vh_ref, o_ref, st_ref, kbuf, vbuf, *, n_heads):
    blk = ATT_BLK
    wd = q_ref.shape[1]
    qb = q_ref.shape[0] // blk
    tile = pl.program_id(2)
    kbuf[0:blk, :] = kh_ref[...]
    kbuf[blk:, :] = k_ref[...]
    vbuf[0:blk, :] = vh_ref[...]
    vbuf[blk:, :] = v_ref[...]

    qi = lax.broadcasted_iota(jnp.int32, (blk, 2 * blk), 0)
    kj = lax.broadcasted_iota(jnp.int32, (blk, 2 * blk), 1)
    rel = blk + qi - kj
    band = (rel >= 0) & (rel <= blk)
    bias_any = jnp.where(band, 0.0, NEG_BIG)
    bias_first = jnp.where(band & (kj >= blk), 0.0, NEG_BIG)
    head_of_lane = lax.broadcasted_iota(jnp.int32, (1, wd), 1) // HEAD_DIM
    st_lane = lax.broadcasted_iota(jnp.int32, (1, st_ref.shape[1]), 1)

    def block(jb, carry):
        r0 = pl.multiple_of(jb * blk, blk)
        bias = jnp.where((tile > 0) | (jb > 0), bias_any, bias_first)
        q = q_ref[pl.ds(r0, blk), :]
        kw = kbuf[pl.ds(r0, 2 * blk), :]
        vw = vbuf[pl.ds(r0, 2 * blk), :]
        probs, vstack = [], []
        stats = jnp.zeros((blk, st_ref.shape[1]), F32)
        for h in range(n_heads):
            sel = head_of_lane == h
            qh = jnp.where(sel, q, jnp.zeros_like(q))
            s = lax.dot_general(qh, kw, (((1,), (1,)), ((), ())), preferred_element_type=F32) + bias
            m = jnp.max(s, axis=-1, keepdims=True)
            p = jnp.exp(s - m)
            l = jnp.sum(p, axis=-1, keepdims=True)
            probs.append((p * (1.0 / l)).astype(BF16))
            vstack.append(jnp.where(sel, vw, jnp.zeros_like(vw)))
            stats = jnp.where(st_lane == h, m + jnp.log(l), stats)
        o = jnp.dot(jnp.concatenate(probs, axis=1), jnp.concatenate(vstack, axis=0),
                    preferred_element_type=F32)
        o_ref[pl.ds(r0, blk), :] = o.astype(BF16)
        st_ref[pl.ds(r0, blk), :] = stats
        return carry

    lax.fori_loop(0, qb, block, 0)


ST_LANES = 128


def _attention(q, k, v, batch, seq, dilation):
    n, wd = q.shape
    n_heads = wd // HEAD_DIM
    rows_ = seq // dilation
    qb = min(8, rows_ // ATT_BLK)
    tiles = rows_ // (qb * ATT_BLK)
    view = lambda a: a.reshape(batch, rows_, dilation * wd)
    cur = pl.BlockSpec((None, qb * ATT_BLK, wd), lambda b, r, i: (b, i, r))
    prev = pl.BlockSpec((None, ATT_BLK, wd), lambda b, r, i: (b, jnp.maximum(i * qb - 1, 0), r))
    o, st = pl.pallas_call(
        functools.partial(_attn_body, n_heads=n_heads),
        grid=(batch, dilation, tiles),
        in_specs=[cur, cur, prev, cur, prev],
        out_specs=[cur, pl.BlockSpec((None, qb * ATT_BLK, ST_LANES), lambda b, r, i: (b, i, r))],
        out_shape=[jax.ShapeDtypeStruct((batch, rows_, dilation * wd), BF16),
                   jax.ShapeDtypeStruct((batch, rows_, dilation * ST_LANES), F32)],
        scratch_shapes=[pltpu.VMEM(((qb + 1) * ATT_BLK, wd), BF16),
                        pltpu.VMEM(((qb + 1) * ATT_BLK, wd), BF16)],
        compiler_params=_cparams("parallel", "parallel", "arbitrary"),
        name=f"attn_d{dilation}",
    )(view(q), view(k), view(k), view(v), view(v))
    return o.reshape(n, wd), st.reshape(n, ST_LANES)


def _att_mix_body(o1_ref, o2_ref, o3_ref, s1_ref, s2_ref, s3_ref, gn_ref, out_ref, *, n_heads):
    lses = [s1_ref[...], s2_ref[...], s3_ref[...]]
    m = jnp.maximum(jnp.maximum(lses[0], lses[1]), lses[2])
    es = [jnp.exp(x - m) for x in lses]
    inv = 1.0 / (es[0] + es[1] + es[2])
    wd = out_ref.shape[1]
    head_of_lane = lax.broadcasted_iota(jnp.int32, (1, wd), 1) // HEAD_DIM
    y = jnp.zeros(out_ref.shape, F32)
    for e, o_ref in zip(es, (o1_ref, o2_ref, o3_ref)):
        wt = e * inv
        wide = jnp.zeros(out_ref.shape, F32)
        for h in range(n_heads):
            wide = jnp.where(head_of_lane == h, wt[:, h:h + 1], wide)
        y = y + wide * o_ref[...].astype(F32)
    out_ref[...] = _rms(y, gn_ref[...]).astype(BF16)


def _att_mix(outs, stats, gn, tm):
    n, wd = outs[0].shape
    row = lambda i: (i, 0)
    return pl.pallas_call(
        functools.partial(_att_mix_body, n_heads=wd // HEAD_DIM),
        grid=(n // tm,),
        in_specs=[pl.BlockSpec((tm, wd), row)] * 3 + [pl.BlockSpec((tm, ST_LANES), row)] * 3
        + [_resident((1, wd))],
        out_specs=pl.BlockSpec((tm, wd), row),
        out_shape=jax.ShapeDtypeStruct((n, wd), BF16),
        compiler_params=_cparams("parallel"),
        name="att_mix",
    )(*outs, *stats, gn.reshape(1, wd))


FF_CHUNK = 256


def _out_ffn_body(x_ref, ycc_ref, ys_ref, ya_ref, wo_ref, gf_ref, wg_ref, wu_ref, wd_ref, gl_ref,
                  o_ref, act, *, final_norm):
    mixed = jnp.concatenate([ycc_ref[...], ys_ref[...], ya_ref[...]], axis=1)
    x1 = x_ref[...] + jnp.dot(mixed, wo_ref[...], preferred_element_type=F32)
    o_ref[...] = x1
    h = _rms(x1, gf_ref[...]).astype(BF16)
    dff = wg_ref.shape[1]
    for c in range(dff // FF_CHUNK):
        sl = slice(c * FF_CHUNK, (c + 1) * FF_CHUNK)
        g = jnp.dot(h, wg_ref[:, sl], preferred_element_type=F32)
        u = jnp.dot(h, wu_ref[:, sl], preferred_element_type=F32)
        act[:, sl] = (g * jax.nn.sigmoid(g) * u).astype(BF16)
    x2 = o_ref[...] + jnp.dot(act[...], wd_ref[...], preferred_element_type=F32)
    if final_norm:
        x2 = _rms(x2, gl_ref[...])
    o_ref[...] = x2


def _out_ffn(x, ycc, ys, ya, wo, gf, wg, wu, wd, gl, final_norm, tm):
    n, d = x.shape
    dff = wg.shape[1]
    row = lambda i: (i, 0)
    return pl.pallas_call(
        functools.partial(_out_ffn_body, final_norm=final_norm),
        grid=(n // tm,),
        in_specs=[pl.BlockSpec((tm, d), row), pl.BlockSpec((tm, ycc.shape[1]), row),
                  pl.BlockSpec((tm, ys.shape[1]), row), pl.BlockSpec((tm, ya.shape[1]), row),
                  _resident(wo.shape), _resident((1, d)), _resident(wg.shape), _resident(wu.shape),
                  _resident(wd.shape), _resident((1, d))],
        out_specs=pl.BlockSpec((tm, d), row),
        out_shape=jax.ShapeDtypeStruct((n, d), F32),
        scratch_shapes=[pltpu.VMEM((tm, dff), BF16)],
        compiler_params=_cparams("parallel"),
        name="out_ffn",
    )(x, ycc, ys, ya, wo, gf.reshape(1, d), wg, wu, wd, gl.reshape(1, d))


def _rope_tables(seq, wd):
    half = HEAD_DIM // 2
    inv = ROPE_THETA ** (-jnp.arange(0, HEAD_DIM, 2, dtype=F32) / HEAD_DIM)
    ang = jnp.arange(seq, dtype=F32)[:, None] * inv[None, :]
    cos, sin = jnp.cos(ang), jnp.sin(ang)
    reps = wd // HEAD_DIM
    cos_t = jnp.tile(jnp.concatenate([cos, cos], axis=1), (1, reps))
    sin_t = jnp.tile(jnp.concatenate([-sin, sin], axis=1), (1, reps))
    scale = HEAD_DIM ** -0.5
    return cos_t * scale, sin_t * scale, cos_t, sin_t


def kernel(x, norm_mix_g, w_in, conv3_w, cfm_dw_w, cfm_dw_b, cfm_ln_g, cfm_ln_b, s5_a_re, s5_a_im, s5_log_dt, s5_b_re, s5_b_im, s5_c_re, s5_c_im, s5_d, s5_glu_w, grp_norm_g, w_out, norm_ffn_g, w_gate, w_up, w_down, final_norm_g):
    batch, seq, d_model = x.shape
    depth = w_in.shape[0]
    wmix = w_in.shape[2] // 9
    n = batch * seq
    assert wmix == 2 * 8 * SSM_CH and all(w // d == ATT_BLK for w, d in DILATED_CFG)
    assert seq % (DILATED_CFG[-1][1] * ATT_BLK) == 0 and seq % 512 == 0
    tm = 512
    scan_steps = max((seq // SSM_CHUNK - 1).bit_length(), 1)
    rope_tabs = _rope_tables(seq, wmix)

    xf = x.reshape(n, d_model)
    for layer in range(depth):
        pc, pf, u, q, k, v = _in_proj(xf, norm_mix_g[layer], w_in[layer].astype(BF16), rope_tabs,
                                      seq, tm)
        gn = grp_norm_g[layer]
        ycc = _conv_mixers(pc, pf, conv3_w[layer], cfm_dw_w[layer], cfm_dw_b[layer],
                           cfm_ln_g[layer], cfm_ln_b[layer], gn[:2 * wmix], batch, seq, 512)
        mats = _ssm_mats(s5_a_re[layer], s5_a_im[layer], s5_log_dt[layer], s5_b_re[layer],
                         s5_b_im[layer], s5_c_re[layer], s5_c_im[layer], scan_steps)
        z = _ssm_core(u, mats, s5_d[layer], batch, seq)
        ys = _ssm_glu(z, s5_glu_w[layer].astype(BF16), gn[2 * wmix:3 * wmix], 1024)
        outs, stats = zip(*[_attention(q, k, v, batch, seq, dil) for _, dil in DILATED_CFG])
        ya = _att_mix(outs, stats, gn[3 * wmix:], 1024)
        xf = _out_ffn(xf, ycc, ys, ya, w_out[layer].astype(BF16), norm_ffn_g[layer],
                      w_gate[layer].astype(BF16), w_up[layer].astype(BF16),
                      w_down[layer].astype(BF16), final_norm_g, layer == depth - 1, tm)
    return xf.reshape(batch, seq, d_model)
```

```python
import functools
import math

import jax
import jax.numpy as jnp
from jax import lax
from jax.experimental import pallas as pl
from jax.experimental.pallas import tpu as pltpu

F32 = jnp.float32
BF16 = jnp.bfloat16

EPS = 1e-6
HEAD_DIM = 64
SHORT_K = 3
CFM_K = 31
SSM_CH = 16
SSM_STATE = 64
SSM_CHUNK = 8
DILATED_CFG = ((128, 1), (512, 4), (2048, 16))
ROPE_THETA = 10000.0
ATT_BLK = 128
NEG_BIG = -1e30
VMEM_LIMIT = 56 * 1024 * 1024


def _cparams(*sem):
    return pltpu.CompilerParams(dimension_semantics=sem, vmem_limit_bytes=VMEM_LIMIT)


def _rms(x, g):
    return x * lax.rsqrt(jnp.mean(x * x, axis=-1, keepdims=True) + EPS) * g


def _resident(shape):
    nd = len(shape)
    return pl.BlockSpec(shape, lambda *_: (0,) * nd, pipeline_mode=pl.Buffered(1))


LANES = 128


def _to_slabs(scr, val):
    for s in range(scr.shape[0]):
        scr[s] = val[:, s * LANES:(s + 1) * LANES]


def _rows_strided(scr, start, count, stride):
    return jnp.concatenate([scr[s, pl.ds(start, count, stride=stride), :]
                            for s in range(scr.shape[0])], axis=1)


def _in_proj_body(x_ref, g_ref, w_ref, cq_ref, sq_ref, ck_ref, sk_ref, pc_ref, pf_ref, u_ref,
                  *rest, wmix, dilations):
    qkv_refs, scr = rest[:-1], rest[-1]
    tm = x_ref.shape[0]
    h = _rms(x_ref[...], g_ref[...]).astype(BF16)

    def proj(lo, width):
        return jnp.dot(h, w_ref[:, lo:lo + width], preferred_element_type=F32)

    pc_ref[...] = proj(0, 3 * wmix).astype(BF16)
    pf_ref[...] = proj(3 * wmix, 2 * wmix).astype(BF16)

    _to_slabs(scr, proj(5 * wmix, wmix))
    for t in range(SSM_CHUNK):
        u_ref[:, t * wmix:(t + 1) * wmix] = _rows_strided(
            scr, t, tm // SSM_CHUNK, SSM_CHUNK).astype(BF16)

    lane = lax.broadcasted_iota(jnp.int32, (1, wmix), 1)
    first_half = (lane % HEAD_DIM) < (HEAD_DIM // 2)

    def rope(t, c_ref, s_ref):
        partner = jnp.where(first_half, pltpu.roll(t, wmix - HEAD_DIM // 2, 1),
                            pltpu.roll(t, HEAD_DIM // 2, 1))
        return t * c_ref[...] + partner * s_ref[...]

    vals = (rope(proj(6 * wmix, wmix), cq_ref, sq_ref), rope(proj(7 * wmix, wmix), ck_ref, sk_ref),
            proj(8 * wmix, wmix))
    for i, val in enumerate(vals):
        outs = qkv_refs[i * len(dilations):(i + 1) * len(dilations)]
        if any(d > 1 for d in dilations):
            _to_slabs(scr, val)
        for d, o_ref in zip(dilations, outs):
            if d == 1:
                o_ref[0] = val.astype(BF16)
            else:
                for r in range(d):
                    o_ref[r] = _rows_strided(scr, r, tm // d, d).astype(BF16)


def _in_proj(x, g, w, rope_tabs, batch, seq, tm, dilations):
    n, d = x.shape
    nin = w.shape[1]
    wmix = nin // 9
    per_seq = seq // tm
    row = lambda i: (i, 0)
    tab = lambda i: (i % per_seq, 0)
    flat = [(n, 3 * wmix), (n, 2 * wmix), (n // SSM_CHUNK, SSM_CHUNK * wmix)]
    flat_blocks = [(tm, 3 * wmix), (tm, 2 * wmix), (tm // SSM_CHUNK, SSM_CHUNK * wmix)]
    res_shapes = [(batch, dl, seq // dl, wmix) for dl in dilations] * 3
    res_specs = [pl.BlockSpec((None, dl, tm // dl, wmix),
                              lambda i: (i // per_seq, 0, i % per_seq, 0)) for dl in dilations] * 3
    return pl.pallas_call(
        functools.partial(_in_proj_body, wmix=wmix, dilations=dilations),
        grid=(n // tm,),
        in_specs=[pl.BlockSpec((tm, d), row), _resident((1, d)), _resident((d, nin))]
        + [pl.BlockSpec((tm, wmix), tab)] * 4,
        out_specs=[pl.BlockSpec(b, row) for b in flat_blocks] + res_specs,
        out_shape=[jax.ShapeDtypeStruct(s, BF16) for s in flat + res_shapes],
        scratch_shapes=[pltpu.VMEM((wmix // LANES, tm, LANES), F32)],
        compiler_params=_cparams("parallel"),
        name="in_proj",
    )(x, g.reshape(1, d), w, *rope_tabs)


CONV_ROWS = 32
CFM_HALO = 32
SHORT_HALO = 16


def _conv_body(pc_ref, pch_ref, pf_ref, pfh_ref, w3_ref, wdw_ref, bdw_ref, lng_ref, lnb_ref,
               gn_ref, o_ref, zb3, zs, *, wmix):
    t_rows = pc_ref.shape[0]
    keep = (pl.program_id(1) > 0).astype(F32)

    pc = pc_ref[...].astype(F32)
    ch = pc[:, 2 * wmix:] * pc[:, :wmix]
    hal = pch_ref[...].astype(F32)
    zb3[0:SHORT_HALO, :] = hal[:, 2 * wmix:] * hal[:, :wmix] * keep
    zb3[SHORT_HALO:, :] = ch
    conv = w3_ref[SHORT_K - 1:SHORT_K, :] * ch
    for k in range(SHORT_K - 1):
        off = SHORT_HALO - (SHORT_K - 1) + k
        conv = conv + w3_ref[k:k + 1, :] * zb3[off:off + t_rows, :]
    y_conv = pc[:, wmix:2 * wmix] * conv
    o_ref[:, :wmix] = _rms(y_conv, gn_ref[:, :wmix]).astype(BF16)

    pf = pf_ref[...].astype(F32)
    ph = pfh_ref[...].astype(F32)
    zs[0, 0:CFM_HALO, :] = ph[:, :wmix] * jax.nn.sigmoid(ph[:, wmix:]) * keep
    zs[0, CFM_HALO:, :] = pf[:, :wmix] * jax.nn.sigmoid(pf[:, wmix:])
    span = t_rows + CFM_HALO - 8
    for b in range(1, 8):
        zs[b, 0:span, :] = zs[0, b:b + span, :]

    base = CFM_HALO - (CFM_K - 1)

    def rows(c, carry):
        r0 = pl.multiple_of(c * CONV_ROWS, CONV_ROWS)
        accs = [bdw_ref[...], None, None, None]
        for k in range(CFM_K):
            a, b = divmod(base + k, 8)
            term = wdw_ref[k:k + 1, :] * zs[b, pl.ds(r0 + 8 * a, CONV_ROWS), :]
            accs[k % 4] = term if accs[k % 4] is None else accs[k % 4] + term
        zb3[pl.ds(r0, CONV_ROWS), :] = (accs[0] + accs[1]) + (accs[2] + accs[3])
        return carry

    lax.fori_loop(0, t_rows // CONV_ROWS, rows, 0, unroll=2)

    acc = zb3[0:t_rows, :]
    mu = jnp.mean(acc, axis=-1, keepdims=True)
    xc = acc - mu
    var = jnp.mean(xc * xc, axis=-1, keepdims=True)
    y = xc * lax.rsqrt(var + EPS) * lng_ref[...] + lnb_ref[...]
    y = y * jax.nn.sigmoid(y)
    o_ref[:, wmix:] = _rms(y, gn_ref[:, wmix:]).astype(BF16)


def _conv_mixers(pc, pf, w3, wdw, bdw, lng, lnb, gn, batch, seq, tt):
    n = pc.shape[0]
    wmix = pf.shape[1] // 2
    per_seq = seq // tt
    row = lambda b, i: (b * per_seq + i, 0)

    def halo(rows_):
        return lambda b, i: (jnp.maximum((b * per_seq + i) * (tt // rows_) - 1, 0), 0)

    vec = lambda a: a.reshape(1, -1)
    return pl.pallas_call(
        functools.partial(_conv_body, wmix=wmix),
        grid=(batch, per_seq),
        in_specs=[pl.BlockSpec((tt, 3 * wmix), row),
                  pl.BlockSpec((SHORT_HALO, 3 * wmix), halo(SHORT_HALO)),
                  pl.BlockSpec((tt, 2 * wmix), row),
                  pl.BlockSpec((CFM_HALO, 2 * wmix), halo(CFM_HALO)),
                  _resident((SHORT_K, wmix)), _resident((CFM_K, wmix)),
                  _resident((1, wmix)), _resident((1, wmix)), _resident((1, wmix)),
                  _resident((1, 2 * wmix))],
        out_specs=pl.BlockSpec((tt, 2 * wmix), row),
        out_shape=jax.ShapeDtypeStruct((n, 2 * wmix), BF16),
        scratch_shapes=[pltpu.VMEM((tt + SHORT_HALO, wmix), F32),
                        pltpu.VMEM((8, tt + CFM_HALO, wmix), F32)],
        compiler_params=_cparams("parallel", "arbitrary"),
        name="conv_mixers",
    )(pc, pc, pf, pf, w3, wdw, vec(bdw), vec(lng), vec(lnb), vec(gn))


def _cmul(ar, ai, br, bi):
    return ar * br - ai * bi, ar * bi + ai * br


def _ssm_mats(a_re, a_im, log_dt, b_re, b_im, c_re, c_im, n_steps):
    hi = lax.Precision.HIGHEST
    ng, ns = a_re.shape
    nh = b_re.shape[-1]
    tc = SSM_CHUNK
    dt = jnp.exp(log_dt)[:, None]
    mag = jnp.exp(a_re * dt)
    lr, li = mag * jnp.cos(a_im * dt), mag * jnp.sin(a_im * dt)
    den = a_re * a_re + a_im * a_im
    nr, ni = lr - 1.0, li
    fr = (nr * a_re + ni * a_im) / den
    fi = (ni * a_re - nr * a_im) / den
    bbr = fr[..., None] * b_re - fi[..., None] * b_im
    bbi = fr[..., None] * b_im + fi[..., None] * b_re

    pr, pi = [jnp.ones_like(lr)], [jnp.zeros_like(li)]
    for _ in range(tc):
        r, i = _cmul(pr[-1], pi[-1], lr, li)
        pr.append(r)
        pi.append(i)
    pwr, pwi = jnp.stack(pr), jnp.stack(pi)

    lbr, lbi = _cmul(pwr[:tc, :, :, None], pwi[:tc, :, :, None], bbr[None], bbi[None])
    kd = (jnp.einsum('ghp,dgpk->dghk', c_re, lbr, precision=hi)
          - jnp.einsum('ghp,dgpk->dghk', c_im, lbi, precision=hi))
    tau = jnp.arange(tc)[:, None]
    tt = jnp.arange(tc)[None, :]
    lag = tt - tau
    kfull = jnp.where((lag >= 0)[:, :, None, None, None], kd[jnp.clip(lag, 0, tc - 1)], 0.0)
    m_full = kfull.transpose(0, 2, 4, 1, 3)
    w_r = lbr[::-1].transpose(0, 1, 3, 2)
    w_i = lbi[::-1].transpose(0, 1, 3, 2)
    clr, cli = _cmul(c_re[None], c_im[None], pwr[1:, :, None, :], pwi[1:, :, None, :])
    g_r = clr.transpose(1, 3, 0, 2)
    g_i = -cli.transpose(1, 3, 0, 2)

    gh = ng // 2
    eye = jnp.eye(gh, dtype=F32)
    ms, ws, gs, aps = [], [], [], []
    for j in range(2):
        sl = slice(j * gh, (j + 1) * gh)
        mj = m_full[:, sl][:, :, :, :, None, :] * eye[None, :, None, None, :, None]
        ms.append(mj.reshape(tc * gh * nh, tc * gh * nh))
        wj = jnp.stack([w_r[:, sl], w_i[:, sl]], axis=3)
        wj = wj[:, :, :, :, None, :] * eye[None, :, None, None, :, None]
        ws.append(wj.reshape(tc * gh * nh, 2 * gh * ns))
        gj = jnp.stack([g_r[sl], g_i[sl]], axis=0)
        gj = gj[:, :, :, :, None, :] * eye[None, :, None, None, :, None]
        gs.append(gj.reshape(2 * gh * ns, tc * gh * nh))
        ar, ai = pwr[tc, sl].reshape(-1), pwi[tc, sl].reshape(-1)
        steps = []
        for _ in range(n_steps):
            steps.append(jnp.concatenate([ar, ai]))
            ar, ai = _cmul(ar, ai, ar, ai)
        aps.append(jnp.stack(steps))
    return (jnp.stack(ms).astype(BF16), jnp.stack(ws).astype(BF16), jnp.stack(gs).astype(BF16),
            jnp.stack(aps))


def _ssm_body(u_ref, m_ref, w_ref, g_ref, ap_ref, d_ref, wglu_ref, gn_ref, o_ref, bre, bim, zrow,
              tok, *, n_steps, half):
    rows_ = u_ref.shape[0]
    pad = bre.shape[0] - rows_
    ch = 2 * half
    nst = ap_ref.shape[2] // 2
    bre[0:pad, :] = jnp.zeros((pad, nst), F32)
    bim[0:pad, :] = jnp.zeros((pad, nst), F32)
    for j in range(2):
        lanes = [slice(t * ch + j * half, t * ch + (j + 1) * half) for t in range(SSM_CHUNK)]
        lhs = jnp.concatenate([u_ref[:, s] for s in lanes], axis=1)
        z = jnp.dot(lhs, w_ref[j], preferred_element_type=F32)
        cr, ci = z[:, :nst], z[:, nst:]
        for k in range(n_steps):
            s = 1 << k
            bre[pad:, :] = cr
            bim[pad:, :] = ci
            sr = bre[pad - s:pad - s + rows_, :]
            si = bim[pad - s:pad - s + rows_, :]
            ar = ap_ref[j, k:k + 1, :nst]
            ai = ap_ref[j, k:k + 1, nst:]
            cr = cr + ar * sr - ai * si
            ci = ci + ar * si + ai * sr
        bre[pad:, :] = cr
        bim[pad:, :] = ci
        s_in = jnp.concatenate([bre[pad - 1:pad - 1 + rows_, :], bim[pad - 1:pad - 1 + rows_, :]],
                               axis=1).astype(BF16)
        y = (jnp.dot(lhs, m_ref[j], preferred_element_type=F32)
             + jnp.dot(s_in, g_ref[j], preferred_element_type=F32))
        for t, s in enumerate(lanes):
            yy = y[:, t * half:(t + 1) * half] + d_ref[:, s] * u_ref[:, s].astype(F32)
            zrow[:, s] = jax.nn.gelu(yy).astype(BF16)

    for t in range(SSM_CHUNK):
        gl = jnp.dot(zrow[:, t * ch:(t + 1) * ch], wglu_ref[...], preferred_element_type=F32)
        y = _rms(gl[:, :ch] * jax.nn.sigmoid(gl[:, ch:]), gn_ref[...])
        for s in range(tok.shape[0]):
            tok[s, pl.ds(t, rows_, stride=SSM_CHUNK), :] = y[:, s * LANES:(s + 1) * LANES]
    o_ref[...] = jnp.concatenate([tok[s] for s in range(tok.shape[0])], axis=1).astype(BF16)


def _ssm_mixer(u_rows, mats, d_skip, wglu, gn, batch, seq):
    width = u_rows.shape[1]
    ch = width // SSM_CHUNK
    m, w, g, ap = mats
    rows_ = seq // SSM_CHUNK
    n_steps = ap.shape[1]
    pad = max(rows_ // 2, 8)
    d_row = jnp.tile(d_skip, SSM_CHUNK).reshape(1, width)
    nst = ap.shape[2] // 2
    return pl.pallas_call(
        functools.partial(_ssm_body, n_steps=n_steps, half=ch // 2),
        grid=(batch,),
        in_specs=[pl.BlockSpec((rows_, width), lambda b: (b, 0)),
                  _resident(m.shape), _resident(w.shape), _resident(g.shape),
                  _resident(ap.shape), _resident((1, width)), _resident(wglu.shape),
                  _resident((1, ch))],
        out_specs=pl.BlockSpec((seq, ch), lambda b: (b, 0)),
        out_shape=jax.ShapeDtypeStruct((batch * seq, ch), BF16),
        scratch_shapes=[pltpu.VMEM((pad + rows_, nst), F32), pltpu.VMEM((pad + rows_, nst), F32),
                        pltpu.VMEM((rows_, width), BF16), pltpu.VMEM((ch // LANES, seq, LANES), F32)],
        compiler_params=_cparams("parallel"),
        name="ssm_mixer",
    )(u_rows, m, w, g, ap, d_row, wglu, gn.reshape(1, ch))


def _attn_body(q_ref, k_ref, kh_ref, v_ref, vh_ref, o_ref, st_ref, kbuf, vbuf, *, n_heads):
    blk = ATT_BLK
    wd = q_ref.shape[1]
    qb = q_ref.shape[0] // blk
    tile = pl.program_id(2)
    kbuf[0:blk, :] = kh_ref[...]
    kbuf[blk:, :] = k_ref[...]
    vbuf[0:blk, :] = vh_ref[...]
    vbuf[blk:, :] = v_ref[...]

    qi = lax.broadcasted_iota(jnp.int32, (blk, 2 * blk), 0)
    kj = lax.broadcasted_iota(jnp.int32, (blk, 2 * blk), 1)
    rel = blk + qi - kj
    band = (rel >= 0) & (rel <= blk)
    bias_any = jnp.where(band, 0.0, NEG_BIG)
    bias_first = jnp.where(band & (kj >= blk), 0.0, NEG_BIG)
    head_of_lane = lax.broadcasted_iota(jnp.int32, (1, wd), 1) // HEAD_DIM
    st_lane = lax.broadcasted_iota(jnp.int32, (1, st_ref.shape[1]), 1)

    def block(jb, carry):
        r0 = pl.multiple_of(jb * blk, blk)
        bias = jnp.where((tile > 0) | (jb > 0), bias_any, bias_first)
        q = q_ref[pl.ds(r0, blk), :]
        kw = kbuf[pl.ds(r0, 2 * blk), :]
        vw = vbuf[pl.ds(r0, 2 * blk), :]
        probs, vstack = [], []
        stats = jnp.zeros((blk, st_ref.shape[1]), F32)
        for h in range(n_heads):
            sel = head_of_lane == h
            qh = jnp.where(sel, q, jnp.zeros_like(q))
            s = lax.dot_general(qh, kw, (((1,), (1,)), ((), ())), preferred_element_type=F32) + bias
            m = jnp.max(s, axis=-1, keepdims=True)
            p = jnp.exp(s - m)
            l = jnp.sum(p, axis=-1, keepdims=True)
            probs.append((p * (1.0 / l)).astype(BF16))
            vstack.append(jnp.where(sel, vw, jnp.zeros_like(vw)))
            stats = jnp.where(st_lane == h, m + jnp.log(l), stats)
        o = jnp.dot(jnp.concatenate(probs, axis=1), jnp.concatenate(vstack, axis=0),
                    preferred_element_type=F32)
        o_ref[pl.ds(r0, blk), :] = o.astype(BF16)
        st_ref[pl.ds(r0, blk), :] = stats
        return carry

    lax.fori_loop(0, qb, block, 0, unroll=2)


ST_LANES = LANES


def _attention(q, k, v):
    batch, dilation, rows_, wd = q.shape
    n_heads = wd // HEAD_DIM
    qb = min(8, rows_ // ATT_BLK)
    tiles = rows_ // (qb * ATT_BLK)
    cur = pl.BlockSpec((None, None, qb * ATT_BLK, wd), lambda b, r, i: (b, r, i, 0))
    prev = pl.BlockSpec((None, None, ATT_BLK, wd),
                        lambda b, r, i: (b, r, jnp.maximum(i * qb - 1, 0), 0))
    return pl.pallas_call(
        functools.partial(_attn_body, n_heads=n_heads),
        grid=(batch, dilation, tiles),
        in_specs=[cur, cur, prev, cur, prev],
        out_specs=[cur, pl.BlockSpec((None, None, qb * ATT_BLK, ST_LANES),
                                     lambda b, r, i: (b, r, i, 0))],
        out_shape=[jax.ShapeDtypeStruct(q.shape, BF16),
                   jax.ShapeDtypeStruct((batch, dilation, rows_, ST_LANES), F32)],
        scratch_shapes=[pltpu.VMEM(((qb + 1) * ATT_BLK, wd), BF16),
                        pltpu.VMEM(((qb + 1) * ATT_BLK, wd), BF16)],
        compiler_params=_cparams("parallel", "parallel", "arbitrary"),
        name=f"attn_d{dilation}",
    )(q, k, k, v, v)


def _att_mix_body(*refs, n_heads, dilations):
    npat = len(dilations)
    o_refs, s_refs = refs[:npat], refs[npat:2 * npat]
    gn_ref, out_ref, oscr, sscr = refs[2 * npat:]
    tm, wd = out_ref.shape
    outs, lses = [], []
    for p, d in enumerate(dilations):
        if d == 1:
            outs.append(o_refs[p][0].astype(F32))
            lses.append(s_refs[p][0])
            continue
        for r in range(d):
            o_r = o_refs[p][r].astype(F32)
            for s in range(wd // LANES):
                oscr[p, s, pl.ds(r, tm // d, stride=d), :] = o_r[:, s * LANES:(s + 1) * LANES]
            sscr[p, pl.ds(r, tm // d, stride=d), :] = s_refs[p][r]
        outs.append(jnp.concatenate([oscr[p, s] for s in range(wd // LANES)], axis=1))
        lses.append(sscr[p])
    m = functools.reduce(jnp.maximum, lses)
    es = [jnp.exp(x - m) for x in lses]
    inv = 1.0 / functools.reduce(lambda a, b: a + b, es)
    head_of_lane = lax.broadcasted_iota(jnp.int32, (1, wd), 1) // HEAD_DIM
    y = jnp.zeros((tm, wd), F32)
    for e, o in zip(es, outs):
        wt = e * inv
        wide = jnp.zeros((tm, wd), F32)
        for h in range(n_heads):
            wide = jnp.where(head_of_lane == h, wt[:, h:h + 1], wide)
        y = y + wide * o
    out_ref[...] = _rms(y, gn_ref[...]).astype(BF16)


def _att_mix(outs, stats, gn, tm):
    batch, _, _, wd = outs[0].shape
    seq = outs[0].shape[1] * outs[0].shape[2]
    dilations = tuple(o.shape[1] for o in outs)
    per_seq = seq // tm
    n = batch * seq

    def res_spec(d, width):
        return pl.BlockSpec((None, d, tm // d, width), lambda i: (i // per_seq, 0, i % per_seq, 0))

    return pl.pallas_call(
        functools.partial(_att_mix_body, n_heads=wd // HEAD_DIM, dilations=dilations),
        grid=(n // tm,),
        in_specs=[res_spec(d, wd) for d in dilations] + [res_spec(d, ST_LANES) for d in dilations]
        + [_resident((1, wd))],
        out_specs=pl.BlockSpec((tm, wd), lambda i: (i, 0)),
        out_shape=jax.ShapeDtypeStruct((n, wd), BF16),
        scratch_shapes=[pltpu.VMEM((len(dilations), wd // LANES, tm, LANES), F32),
                        pltpu.VMEM((len(dilations), tm, ST_LANES), F32)],
        compiler_params=_cparams("parallel"),
        name="att_mix",
    )(*outs, *stats, gn.reshape(1, wd))


FF_CHUNK = 256


def _out_ffn_body(x_ref, ycc_ref, ys_ref, ya_ref, wo_ref, gf_ref, wg_ref, wu_ref, wd_ref, gl_ref,
                  o_ref, act, *, final_norm):
    mixed = jnp.concatenate([ycc_ref[...], ys_ref[...], ya_ref[...]], axis=1)
    x1 = x_ref[...] + jnp.dot(mixed, wo_ref[...], preferred_element_type=F32)
    o_ref[...] = x1
    h = _rms(x1, gf_ref[...]).astype(BF16)
    dff = wg_ref.shape[1]
    for c in range(dff // FF_CHUNK):
        sl = slice(c * FF_CHUNK, (c + 1) * FF_CHUNK)
        g = jnp.dot(h, wg_ref[:, sl], preferred_element_type=F32)
        u = jnp.dot(h, wu_ref[:, sl], preferred_element_type=F32)
        act[:, sl] = (g * jax.nn.sigmoid(g) * u).astype(BF16)
    x2 = o_ref[...] + jnp.dot(act[...], wd_ref[...], preferred_element_type=F32)
    if final_norm:
        x2 = _rms(x2, gl_ref[...])
    o_ref[...] = x2


def _out_ffn(x, ycc, ys, ya, wo, gf, wg, wu, wd, gl, final_norm, tm):
    n, d = x.shape
    dff = wg.shape[1]
    row = lambda i: (i, 0)
    return pl.pallas_call(
        functools.partial(_out_ffn_body, final_norm=final_norm),
        grid=(n // tm,),
        in_specs=[pl.BlockSpec((tm, d), row), pl.BlockSpec((tm, ycc.shape[1]), row),
                  pl.BlockSpec((tm, ys.shape[1]), row), pl.BlockSpec((tm, ya.shape[1]), row),
                  _resident(wo.shape), _resident((1, d)), _resident(wg.shape), _resident(wu.shape),
                  _resident(wd.shape), _resident((1, d))],
        out_specs=pl.BlockSpec((tm, d), row),
        out_shape=jax.ShapeDtypeStruct((n, d), F32),
        scratch_shapes=[pltpu.VMEM((tm, dff), BF16)],
        compiler_params=_cparams("parallel"),
        name="out_ffn",
    )(x, ycc, ys, ya, wo, gf.reshape(1, d), wg, wu, wd, gl.reshape(1, d))


def _rope_tables(seq, wd):
    half = HEAD_DIM // 2
    inv = ROPE_THETA ** (-jnp.arange(0, HEAD_DIM, 2, dtype=F32) / HEAD_DIM)
    ang = jnp.arange(seq, dtype=F32)[:, None] * inv[None, :]
    cos, sin = jnp.cos(ang), jnp.sin(ang)
    reps = wd // HEAD_DIM
    cos_t = jnp.tile(jnp.concatenate([cos, cos], axis=1), (1, reps))
    sin_t = jnp.tile(jnp.concatenate([-sin, sin], axis=1), (1, reps))
    scale = HEAD_DIM ** -0.5
    return cos_t * scale, sin_t * scale, cos_t, sin_t


def kernel(x, norm_mix_g, w_in, conv3_w, cfm_dw_w, cfm_dw_b, cfm_ln_g, cfm_ln_b, s5_a_re, s5_a_im, s5_log_dt, s5_b_re, s5_b_im, s5_c_re, s5_c_im, s5_d, s5_glu_w, grp_norm_g, w_out, norm_ffn_g, w_gate, w_up, w_down, final_norm_g):
    batch, seq, d_model = x.shape
    depth = w_in.shape[0]
    wmix = w_in.shape[2] // 9
    n = batch * seq
    assert wmix == 2 * 8 * SSM_CH and all(w // d == ATT_BLK for w, d in DILATED_CFG)
    assert seq % (DILATED_CFG[-1][1] * ATT_BLK) == 0 and seq % 512 == 0
    tm = 512
    scan_steps = max((seq // SSM_CHUNK - 1).bit_length(), 1)
    rope_tabs = _rope_tables(seq, wmix)

    dilations = tuple(dil for _, dil in DILATED_CFG)
    npat = len(dilations)

    xf = x.reshape(n, d_model)
    for layer in range(depth):
        pc, pf, u, *qkv = _in_proj(xf, norm_mix_g[layer], w_in[layer].astype(BF16), rope_tabs,
                                   batch, seq, tm, dilations)
        gn = grp_norm_g[layer]
        ycc = _conv_mixers(pc, pf, conv3_w[layer], cfm_dw_w[layer], cfm_dw_b[layer],
                           cfm_ln_g[layer], cfm_ln_b[layer], gn[:2 * wmix], batch, seq, 512)
        mats = _ssm_mats(s5_a_re[layer], s5_a_im[layer], s5_log_dt[layer], s5_b_re[layer],
                         s5_b_im[layer], s5_c_re[layer], s5_c_im[layer], scan_steps)
        ys = _ssm_mixer(u, mats, s5_d[layer], s5_glu_w[layer].astype(BF16),
                        gn[2 * wmix:3 * wmix], batch, seq)
        outs, stats = zip(*[_attention(qkv[p], qkv[npat + p], qkv[2 * npat + p])
                            for p in range(npat)])
        ya = _att_mix(outs, stats, gn[3 * wmix:], 1024)
        xf = _out_ffn(xf, ycc, ys, ya, w_out[layer].astype(BF16), norm_ffn_g[layer],
                      w_gate[layer].astype(BF16), w_up[layer].astype(BF16),
                      w_down[layer].astype(BF16), final_norm_g, layer == depth - 1, tm)
    return xf.reshape(batch, seq, d_model)
```

```python
import functools
import math

import jax
import jax.numpy as jnp
from jax import lax
from jax.experimental import pallas as pl
from jax.experimental.pallas import tpu as pltpu

F32 = jnp.float32
BF16 = jnp.bfloat16

EPS = 1e-6
HEAD_DIM = 64
SHORT_K = 3
CFM_K = 31
SSM_CH = 16
SSM_STATE = 64
SSM_CHUNK = 8
DILATED_CFG = ((128, 1), (512, 4), (2048, 16))
ROPE_THETA = 10000.0
ATT_BLK = 128
NEG_BIG = -1e30
VMEM_LIMIT = 56 * 1024 * 1024


def _cparams(*sem):
    return pltpu.CompilerParams(dimension_semantics=sem, vmem_limit_bytes=VMEM_LIMIT)


def _rms(x, g):
    return x * lax.rsqrt(jnp.mean(x * x, axis=-1, keepdims=True) + EPS) * g


def _resident(shape):
    nd = len(shape)
    return pl.BlockSpec(shape, lambda *_: (0,) * nd, pipeline_mode=pl.Buffered(1))


LANES = 128


def _to_slabs(scr, val):
    for s in range(scr.shape[0]):
        scr[s] = val[:, s * LANES:(s + 1) * LANES]


def _rows_strided(scr, start, count, stride):
    return jnp.concatenate([scr[s, pl.ds(start, count, stride=stride), :]
                            for s in range(scr.shape[0])], axis=1)


def _in_proj_body(x_ref, g_ref, w_ref, cq_ref, sq_ref, ck_ref, sk_ref, pc_ref, pf_ref, u_ref,
                  *rest, wmix, dilations):
    qkv_refs, scr = rest[:-1], rest[-1]
    tm = x_ref.shape[0]
    h = _rms(x_ref[...], g_ref[...]).astype(BF16)

    def proj(lo, width):
        return jnp.dot(h, w_ref[:, lo:lo + width], preferred_element_type=F32)

    pc_ref[...] = proj(0, 3 * wmix).astype(BF16)
    pf_ref[...] = proj(3 * wmix, 2 * wmix).astype(BF16)

    _to_slabs(scr.at[3], proj(5 * wmix, wmix))
    for t in range(SSM_CHUNK):
        u_ref[:, t * wmix:(t + 1) * wmix] = _rows_strided(
            scr.at[3], t, tm // SSM_CHUNK, SSM_CHUNK).astype(BF16)

    lane = lax.broadcasted_iota(jnp.int32, (1, wmix), 1)
    first_half = (lane % HEAD_DIM) < (HEAD_DIM // 2)

    def rope(t, c_ref, s_ref):
        partner = jnp.where(first_half, pltpu.roll(t, wmix - HEAD_DIM // 2, 1),
                            pltpu.roll(t, HEAD_DIM // 2, 1))
        return t * c_ref[...] + partner * s_ref[...]

    vals = (rope(proj(6 * wmix, wmix), cq_ref, sq_ref), rope(proj(7 * wmix, wmix), ck_ref, sk_ref),
            proj(8 * wmix, wmix))
    for i, val in enumerate(vals):
        outs = qkv_refs[i * len(dilations):(i + 1) * len(dilations)]
        if any(d > 1 for d in dilations):
            _to_slabs(scr.at[i], val)
        for d, o_ref in zip(dilations, outs):
            if d == 1:
                o_ref[0] = val.astype(BF16)
            else:
                for r in range(d):
                    o_ref[r] = _rows_strided(scr.at[i], r, tm // d, d).astype(BF16)


def _in_proj(x, g, w, rope_tabs, batch, seq, tm, dilations):
    n, d = x.shape
    nin = w.shape[1]
    wmix = nin // 9
    per_seq = seq // tm
    row = lambda i: (i, 0)
    tab = lambda i: (i % per_seq, 0)
    flat = [(n, 3 * wmix), (n, 2 * wmix), (n // SSM_CHUNK, SSM_CHUNK * wmix)]
    flat_blocks = [(tm, 3 * wmix), (tm, 2 * wmix), (tm // SSM_CHUNK, SSM_CHUNK * wmix)]
    res_shapes = [(batch, dl, seq // dl, wmix) for dl in dilations] * 3
    res_specs = [pl.BlockSpec((None, dl, tm // dl, wmix),
                              lambda i: (i // per_seq, 0, i % per_seq, 0)) for dl in dilations] * 3
    return pl.pallas_call(
        functools.partial(_in_proj_body, wmix=wmix, dilations=dilations),
        grid=(n // tm,),
        in_specs=[pl.BlockSpec((tm, d), row), _resident((1, d)), _resident((d, nin))]
        + [pl.BlockSpec((tm, wmix), tab)] * 4,
        out_specs=[pl.BlockSpec(b, row) for b in flat_blocks] + res_specs,
        out_shape=[jax.ShapeDtypeStruct(s, BF16) for s in flat + res_shapes],
        scratch_shapes=[pltpu.VMEM((4, wmix // LANES, tm, LANES), F32)],
        compiler_params=_cparams("parallel"),
        name="in_proj",
    )(x, g.reshape(1, d), w, *rope_tabs)


CONV_ROWS = 32
CFM_HALO = 32
SHORT_HALO = 16


def _conv_body(pc_ref, pch_ref, pf_ref, pfh_ref, w3_ref, wdw_ref, bdw_ref, lng_ref, lnb_ref,
               gn_ref, o_ref, zb3, zs, *, wmix):
    t_rows = pc_ref.shape[0]
    keep = (pl.program_id(1) > 0).astype(F32)

    pc = pc_ref[...].astype(F32)
    ch = pc[:, 2 * wmix:] * pc[:, :wmix]
    hal = pch_ref[...].astype(F32)
    zb3[0:SHORT_HALO, :] = hal[:, 2 * wmix:] * hal[:, :wmix] * keep
    zb3[SHORT_HALO:, :] = ch
    conv = w3_ref[SHORT_K - 1:SHORT_K, :] * ch
    for k in range(SHORT_K - 1):
        off = SHORT_HALO - (SHORT_K - 1) + k
        conv = conv + w3_ref[k:k + 1, :] * zb3[off:off + t_rows, :]
    y_conv = pc[:, wmix:2 * wmix] * conv
    o_ref[:, :wmix] = _rms(y_conv, gn_ref[:, :wmix]).astype(BF16)

    pf = pf_ref[...].astype(F32)
    ph = pfh_ref[...].astype(F32)
    zs[0, 0:CFM_HALO, :] = ph[:, :wmix] * jax.nn.sigmoid(ph[:, wmix:]) * keep
    zs[0, CFM_HALO:, :] = pf[:, :wmix] * jax.nn.sigmoid(pf[:, wmix:])
    span = t_rows + CFM_HALO - 8
    for b in range(1, 8):
        zs[b, 0:span, :] = zs[0, b:b + span, :]

    base = CFM_HALO - (CFM_K - 1)

    def rows(c, carry):
        r0 = pl.multiple_of(c * CONV_ROWS, CONV_ROWS)
        accs = [None] * 4
        for k in range(CFM_K):
            a, b = divmod(base + k, 8)
            taps = zs[b, pl.ds(r0 + 8 * a, CONV_ROWS), :].reshape(CONV_ROWS // 8, 8, wmix)
            term = wdw_ref[k] * taps
            accs[k % 4] = term if accs[k % 4] is None else accs[k % 4] + term
        acc = (accs[0] + accs[1]) + (accs[2] + accs[3])
        zb3[pl.ds(r0, CONV_ROWS), :] = acc.reshape(CONV_ROWS, wmix) + bdw_ref[...]
        return carry

    lax.fori_loop(0, t_rows // CONV_ROWS, rows, 0, unroll=2)

    acc = zb3[0:t_rows, :]
    mu = jnp.mean(acc, axis=-1, keepdims=True)
    xc = acc - mu
    var = jnp.mean(xc * xc, axis=-1, keepdims=True)
    y = xc * lax.rsqrt(var + EPS) * lng_ref[...] + lnb_ref[...]
    y = y * jax.nn.sigmoid(y)
    o_ref[:, wmix:] = _rms(y, gn_ref[:, wmix:]).astype(BF16)


def _conv_mixers(pc, pf, w3, wdw, bdw, lng, lnb, gn, batch, seq, tt):
    n = pc.shape[0]
    wmix = pf.shape[1] // 2
    per_seq = seq // tt
    row = lambda b, i: (b * per_seq + i, 0)

    def halo(rows_):
        return lambda b, i: (jnp.maximum((b * per_seq + i) * (tt // rows_) - 1, 0), 0)

    vec = lambda a: a.reshape(1, -1)
    return pl.pallas_call(
        functools.partial(_conv_body, wmix=wmix),
        grid=(batch, per_seq),
        in_specs=[pl.BlockSpec((tt, 3 * wmix), row),
                  pl.BlockSpec((SHORT_HALO, 3 * wmix), halo(SHORT_HALO)),
                  pl.BlockSpec((tt, 2 * wmix), row),
                  pl.BlockSpec((CFM_HALO, 2 * wmix), halo(CFM_HALO)),
                  _resident((SHORT_K, wmix)), _resident((CFM_K, 8, wmix)),
                  _resident((1, wmix)), _resident((1, wmix)), _resident((1, wmix)),
                  _resident((1, 2 * wmix))],
        out_specs=pl.BlockSpec((tt, 2 * wmix), row),
        out_shape=jax.ShapeDtypeStruct((n, 2 * wmix), BF16),
        scratch_shapes=[pltpu.VMEM((tt + SHORT_HALO, wmix), F32),
                        pltpu.VMEM((8, tt + CFM_HALO, wmix), F32)],
        compiler_params=_cparams("parallel", "arbitrary"),
        name="conv_mixers",
    )(pc, pc, pf, pf, w3, jnp.broadcast_to(wdw[:, None, :], (CFM_K, 8, wmix)), vec(bdw), vec(lng),
      vec(lnb), vec(gn))


def _cmul(ar, ai, br, bi):
    return ar * br - ai * bi, ar * bi + ai * br


def _ssm_mats(a_re, a_im, log_dt, b_re, b_im, c_re, c_im, n_steps):
    hi = lax.Precision.HIGHEST
    ng, ns = a_re.shape
    nh = b_re.shape[-1]
    tc = SSM_CHUNK
    dt = jnp.exp(log_dt)[:, None]
    mag = jnp.exp(a_re * dt)
    lr, li = mag * jnp.cos(a_im * dt), mag * jnp.sin(a_im * dt)
    den = a_re * a_re + a_im * a_im
    nr, ni = lr - 1.0, li
    fr = (nr * a_re + ni * a_im) / den
    fi = (ni * a_re - nr * a_im) / den
    bbr = fr[..., None] * b_re - fi[..., None] * b_im
    bbi = fr[..., None] * b_im + fi[..., None] * b_re

    pr, pi = [jnp.ones_like(lr)], [jnp.zeros_like(li)]
    for _ in range(tc):
        r, i = _cmul(pr[-1], pi[-1], lr, li)
        pr.append(r)
        pi.append(i)
    pwr, pwi = jnp.stack(pr), jnp.stack(pi)

    lbr, lbi = _cmul(pwr[:tc, :, :, None], pwi[:tc, :, :, None], bbr[None], bbi[None])
    kd = (jnp.einsum('ghp,dgpk->dghk', c_re, lbr, precision=hi)
          - jnp.einsum('ghp,dgpk->dghk', c_im, lbi, precision=hi))
    tau = jnp.arange(tc)[:, None]
    tt = jnp.arange(tc)[None, :]
    lag = tt - tau
    kfull = jnp.where((lag >= 0)[:, :, None, None, None], kd[jnp.clip(lag, 0, tc - 1)], 0.0)
    m_full = kfull.transpose(0, 2, 4, 1, 3)
    w_r = lbr[::-1].transpose(0, 1, 3, 2)
    w_i = lbi[::-1].transpose(0, 1, 3, 2)
    clr, cli = _cmul(c_re[None], c_im[None], pwr[1:, :, None, :], pwi[1:, :, None, :])
    g_r = clr.transpose(1, 3, 0, 2)
    g_i = -cli.transpose(1, 3, 0, 2)

    gh = ng // 2
    eye = jnp.eye(gh, dtype=F32)
    ms, ws, gs, aps = [], [], [], []
    for j in range(2):
        sl = slice(j * gh, (j + 1) * gh)
        mj = m_full[:, sl][:, :, :, :, None, :] * eye[None, :, None, None, :, None]
        ms.append(mj.reshape(tc * gh * nh, tc * gh * nh))
        wj = jnp.stack([w_r[:, sl], w_i[:, sl]], axis=3)
        wj = wj[:, :, :, :, None, :] * eye[None, :, None, None, :, None]
        ws.append(wj.reshape(tc * gh * nh, 2 * gh * ns))
        gj = jnp.stack([g_r[sl], g_i[sl]], axis=0)
        gj = gj[:, :, :, :, None, :] * eye[None, :, None, None, :, None]
        gs.append(gj.reshape(2 * gh * ns, tc * gh * nh))
        ar, ai = pwr[tc, sl].reshape(-1), pwi[tc, sl].reshape(-1)
        steps = []
        for _ in range(n_steps):
            steps.append(jnp.concatenate([ar, ai]))
            ar, ai = _cmul(ar, ai, ar, ai)
        aps.append(jnp.stack(steps))
    return (jnp.stack(ms).astype(BF16), jnp.stack(ws).astype(BF16), jnp.stack(gs).astype(BF16),
            jnp.stack(aps))


def _ssm_body(u_ref, m_ref, w_ref, g_ref, ap_ref, d_ref, wglu_ref, gn_ref, o_ref, bre, bim, zrow,
              tok, *, n_steps, half):
    rows_ = u_ref.shape[0]
    pad = bre.shape[0] - rows_
    ch = 2 * half
    nst = ap_ref.shape[2] // 2
    bre[0:pad, :] = jnp.zeros((pad, nst), F32)
    bim[0:pad, :] = jnp.zeros((pad, nst), F32)
    for j in range(2):
        lanes = [slice(t * ch + j * half, t * ch + (j + 1) * half) for t in range(SSM_CHUNK)]
        lhs = jnp.concatenate([u_ref[:, s] for s in lanes], axis=1)
        z = jnp.dot(lhs, w_ref[j], preferred_element_type=F32)
        cr, ci = z[:, :nst], z[:, nst:]
        for k in range(n_steps):
            s = 1 << k
            bre[pad:, :] = cr
            bim[pad:, :] = ci
            sr = bre[pad - s:pad - s + rows_, :]
            si = bim[pad - s:pad - s + rows_, :]
            ar = ap_ref[j, k:k + 1, :nst]
            ai = ap_ref[j, k:k + 1, nst:]
            cr = cr + ar * sr - ai * si
            ci = ci + ar * si + ai * sr
        bre[pad:, :] = cr
        bim[pad:, :] = ci
        s_in = jnp.concatenate([bre[pad - 1:pad - 1 + rows_, :], bim[pad - 1:pad - 1 + rows_, :]],
                               axis=1).astype(BF16)
        y = (jnp.dot(lhs, m_ref[j], preferred_element_type=F32)
             + jnp.dot(s_in, g_ref[j], preferred_element_type=F32))
        for t, s in enumerate(lanes):
            yy = y[:, t * half:(t + 1) * half] + d_ref[:, s] * u_ref[:, s].astype(F32)
            zrow[:, s] = jax.nn.gelu(yy).astype(BF16)

    for t in range(SSM_CHUNK):
        gl = jnp.dot(zrow[:, t * ch:(t + 1) * ch], wglu_ref[...], preferred_element_type=F32)
        y = _rms(gl[:, :ch] * jax.nn.sigmoid(gl[:, ch:]), gn_ref[...])
        for s in range(tok.shape[0]):
            tok[s, pl.ds(t, rows_, stride=SSM_CHUNK), :] = y[:, s * LANES:(s + 1) * LANES]
    o_ref[...] = jnp.concatenate([tok[s] for s in range(tok.shape[0])], axis=1).astype(BF16)


def _ssm_mixer(u_rows, mats, d_skip, wglu, gn, batch, seq):
    width = u_rows.shape[1]
    ch = width // SSM_CHUNK
    m, w, g, ap = mats
    rows_ = seq // SSM_CHUNK
    n_steps = ap.shape[1]
    pad = max(rows_ // 2, 8)
    d_row = jnp.tile(d_skip, SSM_CHUNK).reshape(1, width)
    nst = ap.shape[2] // 2
    return pl.pallas_call(
        functools.partial(_ssm_body, n_steps=n_steps, half=ch // 2),
        grid=(batch,),
        in_specs=[pl.BlockSpec((rows_, width), lambda b: (b, 0)),
                  _resident(m.shape), _resident(w.shape), _resident(g.shape),
                  _resident(ap.shape), _resident((1, width)), _resident(wglu.shape),
                  _resident((1, ch))],
        out_specs=pl.BlockSpec((seq, ch), lambda b: (b, 0)),
        out_shape=jax.ShapeDtypeStruct((batch * seq, ch), BF16),
        scratch_shapes=[pltpu.VMEM((pad + rows_, nst), F32), pltpu.VMEM((pad + rows_, nst), F32),
                        pltpu.VMEM((rows_, width), BF16), pltpu.VMEM((ch // LANES, seq, LANES), F32)],
        compiler_params=_cparams("parallel"),
        name="ssm_mixer",
    )(u_rows, m, w, g, ap, d_row, wglu, gn.reshape(1, ch))


def _attn_body(q_ref, k_ref, v_ref, o_ref, st_ref, sbuf, pbuf, *, n_heads, qb):
    blk = ATT_BLK
    wd = q_ref.shape[1]
    nblk = q_ref.shape[0] // blk

    qi = lax.broadcasted_iota(jnp.int32, (blk, 2 * blk), 0)
    kj = lax.broadcasted_iota(jnp.int32, (blk, 2 * blk), 1)
    rel = blk + qi - kj
    bias_any = jnp.where((rel >= 0) & (rel <= blk), 0.0, NEG_BIG)
    bias_first = jnp.where(kj <= qi, 0.0, NEG_BIG)
    head_of_lane = lax.broadcasted_iota(jnp.int32, (1, wd), 1) // HEAD_DIM
    st_lane = lax.broadcasted_iota(jnp.int32, (1, st_ref.shape[1]), 1)

    def rows_of(f):
        return f * blk if isinstance(f, int) else pl.multiple_of(f * blk, blk)

    def window_of(f):
        first = (f % qb) == 0
        if isinstance(f, int):
            return (f if first else f - 1) * blk, first
        return pl.multiple_of((f - jnp.where(first, 0, 1)) * blk, blk), first

    def scores(f, slot):
        q = q_ref[pl.ds(rows_of(f), blk), :]
        qs = jnp.concatenate([jnp.where(head_of_lane == h, q, jnp.zeros_like(q))
                              for h in range(n_heads)], axis=0)
        sbuf[slot] = lax.dot_general(qs, k_ref[pl.ds(window_of(f)[0], 2 * blk), :],
                                     (((1,), (1,)), ((), ())), preferred_element_type=F32)

    def softmax(f, slot):
        first = window_of(f)[1]
        bias = (bias_first if first else bias_any) if isinstance(f, int) else jnp.where(
            first, bias_first, bias_any)
        stats = jnp.zeros((blk, st_ref.shape[1]), F32)
        for h in range(n_heads):
            s = sbuf[slot, h * blk:(h + 1) * blk, :] + bias
            m = jnp.max(s, axis=-1, keepdims=True)
            p = jnp.exp(s - m)
            pbuf[slot, h * blk:(h + 1) * blk, :] = p.astype(BF16)
            stats = jnp.where(st_lane == h, m, stats)
            stats = jnp.where(st_lane == n_heads + h, jnp.sum(p, axis=-1, keepdims=True), stats)
        st_ref[pl.ds(rows_of(f), blk), :] = stats

    def values(f, slot):
        o_all = jnp.dot(pbuf[slot], v_ref[pl.ds(window_of(f)[0], 2 * blk), :],
                        preferred_element_type=F32)
        o = o_all[0:blk]
        for h in range(1, n_heads):
            o = jnp.where(head_of_lane == h, o_all[h * blk:(h + 1) * blk], o)
        o_ref[pl.ds(rows_of(f), blk), :] = o.astype(BF16)

    def step(f, slot):
        scores(f, slot)
        softmax(f - 1, 1 - slot)
        values(f - 2, slot)

    scores(0, 0)
    scores(1, 1)
    softmax(0, 0)

    def pair(it, carry):
        step(2 * it + 2, 0)
        step(2 * it + 3, 1)
        return carry

    lax.fori_loop(0, (nblk - 2) // 2, pair, 0)
    softmax(nblk - 1, 1)
    values(nblk - 2, 0)
    values(nblk - 1, 1)


ST_LANES = LANES


def _attention(q, k, v):
    batch, dilation, rows_, wd = q.shape
    seq = dilation * rows_
    n_heads = wd // HEAD_DIM
    qb = rows_ // ATT_BLK
    assert qb >= 2 and qb & (qb - 1) == 0
    flat = lambda a: a.reshape(batch, seq, wd)
    spec = pl.BlockSpec((None, seq, wd), lambda b: (b, 0, 0))
    o, st = pl.pallas_call(
        functools.partial(_attn_body, n_heads=n_heads, qb=qb),
        grid=(batch,),
        in_specs=[spec, spec, spec],
        out_specs=[spec, pl.BlockSpec((None, seq, ST_LANES), lambda b: (b, 0, 0))],
        out_shape=[jax.ShapeDtypeStruct((batch, seq, wd), BF16),
                   jax.ShapeDtypeStruct((batch, seq, ST_LANES), F32)],
        scratch_shapes=[pltpu.VMEM((2, n_heads * ATT_BLK, 2 * ATT_BLK), F32),
                        pltpu.VMEM((2, n_heads * ATT_BLK, 2 * ATT_BLK), BF16)],
        compiler_params=_cparams("parallel"),
        name=f"attn_d{dilation}",
    )(flat(q), flat(k), flat(v))
    return (o.reshape(batch, dilation, rows_, wd), st.reshape(batch, dilation, rows_, ST_LANES))


def _att_mix_body(*refs, n_heads, dilations):
    npat = len(dilations)
    o_refs, s_refs = refs[:npat], refs[npat:2 * npat]
    gn_ref, out_ref, oscr, sscr = refs[2 * npat:]
    tm, wd = out_ref.shape
    outs, lses = [], []
    for p, d in enumerate(dilations):
        if d == 1:
            outs.append(o_refs[p][0].astype(F32))
            lses.append(s_refs[p][0])
            continue
        for r in range(d):
            o_r = o_refs[p][r].astype(F32)
            for s in range(wd // LANES):
                oscr[p, s, pl.ds(r, tm // d, stride=d), :] = o_r[:, s * LANES:(s + 1) * LANES]
            sscr[p, pl.ds(r, tm // d, stride=d), :] = s_refs[p][r]
        outs.append(jnp.concatenate([oscr[p, s] for s in range(wd // LANES)], axis=1))
        lses.append(sscr[p])
    m = functools.reduce(jnp.maximum, lses)
    row = lax.broadcasted_iota(jnp.int32, (2 * ST_LANES, 2 * wd), 0) % ST_LANES
    col = lax.broadcasted_iota(jnp.int32, (2 * ST_LANES, 2 * wd), 1)
    spread = jnp.where(col // HEAD_DIM == row, 1.0, 0.0).astype(BF16)
    is_max = lax.broadcasted_iota(jnp.int32, (1, ST_LANES), 1) < n_heads
    num = jnp.zeros((tm, wd), F32)
    den = jnp.zeros((tm, wd), F32)
    for st, o in zip(lses, outs):
        x = jnp.where(is_max, jnp.exp(st - m), st)
        hi = x.astype(BF16)
        lo = (x - hi.astype(F32)).astype(BF16)
        wide = jnp.dot(jnp.concatenate([hi, lo], axis=1), spread, preferred_element_type=F32)
        num = num + wide[:, :wd] * o
        den = den + wide[:, :wd] * wide[:, wd:]
    out_ref[...] = _rms(num / den, gn_ref[...]).astype(BF16)


def _att_mix(outs, stats, gn, tm):
    batch, _, _, wd = outs[0].shape
    seq = outs[0].shape[1] * outs[0].shape[2]
    dilations = tuple(o.shape[1] for o in outs)
    per_seq = seq // tm
    n = batch * seq

    def res_spec(d, width):
        return pl.BlockSpec((None, d, tm // d, width), lambda i: (i // per_seq, 0, i % per_seq, 0))

    return pl.pallas_call(
        functools.partial(_att_mix_body, n_heads=wd // HEAD_DIM, dilations=dilations),
        grid=(n // tm,),
        in_specs=[res_spec(d, wd) for d in dilations] + [res_spec(d, ST_LANES) for d in dilations]
        + [_resident((1, wd))],
        out_specs=pl.BlockSpec((tm, wd), lambda i: (i, 0)),
        out_shape=jax.ShapeDtypeStruct((n, wd), BF16),
        scratch_shapes=[pltpu.VMEM((len(dilations), wd // LANES, tm, LANES), F32),
                        pltpu.VMEM((len(dilations), tm, ST_LANES), F32)],
        compiler_params=_cparams("parallel"),
        name="att_mix",
    )(*outs, *stats, gn.reshape(1, wd))


FF_CHUNK = 256


def _out_ffn_body(x_ref, ycc_ref, ys_ref, ya_ref, wo_ref, gf_ref, wg_ref, wu_ref, wd_ref, gl_ref,
                  o_ref, act, *, final_norm):
    mixed = jnp.concatenate([ycc_ref[...], ys_ref[...], ya_ref[...]], axis=1)
    x1 = x_ref[...] + jnp.dot(mixed, wo_ref[...], preferred_element_type=F32)
    o_ref[...] = x1
    h = _rms(x1, gf_ref[...]).astype(BF16)
    dff = wg_ref.shape[1]
    for c in range(dff // FF_CHUNK):
        sl = slice(c * FF_CHUNK, (c + 1) * FF_CHUNK)
        g = jnp.dot(h, wg_ref[:, sl], preferred_element_type=F32)
        u = jnp.dot(h, wu_ref[:, sl], preferred_element_type=F32)
        act[:, sl] = (g * jax.nn.sigmoid(g) * u).astype(BF16)
    x2 = o_ref[...] + jnp.dot(act[...], wd_ref[...], preferred_element_type=F32)
    if final_norm:
        x2 = _rms(x2, gl_ref[...])
    o_ref[...] = x2


def _out_ffn(x, ycc, ys, ya, wo, gf, wg, wu, wd, gl, final_norm, tm):
    n, d = x.shape
    dff = wg.shape[1]
    row = lambda i: (i, 0)
    return pl.pallas_call(
        functools.partial(_out_ffn_body, final_norm=final_norm),
        grid=(n // tm,),
        in_specs=[pl.BlockSpec((tm, d), row), pl.BlockSpec((tm, ycc.shape[1]), row),
                  pl.BlockSpec((tm, ys.shape[1]), row), pl.BlockSpec((tm, ya.shape[1]), row),
                  _resident(wo.shape), _resident((1, d)), _resident(wg.shape), _resident(wu.shape),
                  _resident(wd.shape), _resident((1, d))],
        out_specs=pl.BlockSpec((tm, d), row),
        out_shape=jax.ShapeDtypeStruct((n, d), F32),
        scratch_shapes=[pltpu.VMEM((tm, dff), BF16)],
        compiler_params=_cparams("parallel"),
        name="out_ffn",
    )(x, ycc, ys, ya, wo, gf.reshape(1, d), wg, wu, wd, gl.reshape(1, d))


def _rope_tables(seq, wd):
    half = HEAD_DIM // 2
    inv = ROPE_THETA ** (-jnp.arange(0, HEAD_DIM, 2, dtype=F32) / HEAD_DIM)
    ang = jnp.arange(seq, dtype=F32)[:, None] * inv[None, :]
    cos, sin = jnp.cos(ang), jnp.sin(ang)
    reps = wd // HEAD_DIM
    cos_t = jnp.tile(jnp.concatenate([cos, cos], axis=1), (1, reps))
    sin_t = jnp.tile(jnp.concatenate([-sin, sin], axis=1), (1, reps))
    scale = HEAD_DIM ** -0.5
    return cos_t * scale, sin_t * scale, cos_t, sin_t


def kernel(x, norm_mix_g, w_in, conv3_w, cfm_dw_w, cfm_dw_b, cfm_ln_g, cfm_ln_b, s5_a_re, s5_a_im, s5_log_dt, s5_b_re, s5_b_im, s5_c_re, s5_c_im, s5_d, s5_glu_w, grp_norm_g, w_out, norm_ffn_g, w_gate, w_up, w_down, final_norm_g):
    batch, seq, d_model = x.shape
    depth = w_in.shape[0]
    wmix = w_in.shape[2] // 9
    n = batch * seq
    assert wmix == 2 * 8 * SSM_CH and all(w // d == ATT_BLK for w, d in DILATED_CFG)
    assert seq % (DILATED_CFG[-1][1] * ATT_BLK) == 0 and seq % 512 == 0
    tm = 512
    scan_steps = max((seq // SSM_CHUNK - 1).bit_length(), 1)
    rope_tabs = _rope_tables(seq, wmix)

    dilations = tuple(dil for _, dil in DILATED_CFG)
    npat = len(dilations)

    xf = x.reshape(n, d_model)
    for layer in range(depth):
        pc, pf, u, *qkv = _in_proj(xf, norm_mix_g[layer], w_in[layer].astype(BF16), rope_tabs,
                                   batch, seq, tm, dilations)
        gn = grp_norm_g[layer]
        ycc = _conv_mixers(pc, pf, conv3_w[layer], cfm_dw_w[layer], cfm_dw_b[layer],
                           cfm_ln_g[layer], cfm_ln_b[layer], gn[:2 * wmix], batch, seq, 512)
        mats = _ssm_mats(s5_a_re[layer], s5_a_im[layer], s5_log_dt[layer], s5_b_re[layer],
                         s5_b_im[layer], s5_c_re[layer], s5_c_im[layer], scan_steps)
        ys = _ssm_mixer(u, mats, s5_d[layer], s5_glu_w[layer].astype(BF16),
                        gn[2 * wmix:3 * wmix], batch, seq)
        outs, stats = zip(*[_attention(qkv[p], qkv[npat + p], qkv[2 * npat + p])
                            for p in range(npat)])
        ya = _att_mix(outs, stats, gn[3 * wmix:], 1024)
        xf = _out_ffn(xf, ycc, ys, ya, w_out[layer].astype(BF16), norm_ffn_g[layer],
                      w_gate[layer].astype(BF16), w_up[layer].astype(BF16),
                      w_down[layer].astype(BF16), final_norm_g, layer == depth - 1, tm)
    return xf.reshape(batch, seq, d_model)
```

```python
import functools
import math

import jax
import jax.numpy as jnp
from jax import lax
from jax.experimental import pallas as pl
from jax.experimental.pallas import tpu as pltpu

F32 = jnp.float32
BF16 = jnp.bfloat16

EPS = 1e-6
HEAD_DIM = 64
SHORT_K = 3
CFM_K = 31
SSM_CH = 16
SSM_STATE = 64
SSM_CHUNK = 8
DILATED_CFG = ((128, 1), (512, 4), (2048, 16))
ROPE_THETA = 10000.0
ATT_BLK = 128
NEG_BIG = -1e30
VMEM_LIMIT = 56 * 1024 * 1024


def _cparams(*sem):
    return pltpu.CompilerParams(dimension_semantics=sem, vmem_limit_bytes=VMEM_LIMIT)


def _rms(x, g):
    return x * lax.rsqrt(jnp.mean(x * x, axis=-1, keepdims=True) + EPS) * g


def _resident(shape):
    nd = len(shape)
    return pl.BlockSpec(shape, lambda *_: (0,) * nd, pipeline_mode=pl.Buffered(1))


LANES = 128


def _to_slabs(scr, val):
    for s in range(scr.shape[0]):
        scr[s] = val[:, s * LANES:(s + 1) * LANES]


def _rows_strided(scr, start, count, stride):
    return jnp.concatenate([scr[s, pl.ds(start, count, stride=stride), :]
                            for s in range(scr.shape[0])], axis=1)


def _in_proj_body(x_ref, g_ref, w_ref, cq_ref, sq_ref, ck_ref, sk_ref, pc_ref, pf_ref, u_ref,
                  *rest, wmix, dilations):
    qkv_refs, scr = rest[:-1], rest[-1]
    tm = x_ref.shape[0]
    h = _rms(x_ref[...], g_ref[...]).astype(BF16)

    def proj(lo, width):
        return jnp.dot(h, w_ref[:, lo:lo + width], preferred_element_type=F32)

    _to_slabs(scr.at[3, 0], proj(5 * wmix, wmix))
    for t in range(SSM_CHUNK):
        u_ref[:, t * wmix:(t + 1) * wmix] = _rows_strided(
            scr.at[3, 0], t, tm // SSM_CHUNK, SSM_CHUNK).astype(BF16)

    lane = lax.broadcasted_iota(jnp.int32, (1, wmix), 1)
    first_half = (lane % HEAD_DIM) < (HEAD_DIM // 2)

    def rope(t, c_ref, s_ref):
        partner = jnp.where(first_half, pltpu.roll(t, wmix - HEAD_DIM // 2, 1),
                            pltpu.roll(t, HEAD_DIM // 2, 1))
        return t * c_ref[...] + partner * s_ref[...]

    vals = (rope(proj(6 * wmix, wmix), cq_ref, sq_ref), rope(proj(7 * wmix, wmix), ck_ref, sk_ref),
            proj(8 * wmix, wmix))
    for i, val in enumerate(vals):
        outs = qkv_refs[i * len(dilations):(i + 1) * len(dilations)]
        prev, stage = 1, 0
        for n_d, (d, o_ref) in enumerate(zip(dilations, outs)):
            if d == 1:
                o_ref[0] = val.astype(BF16)
                continue
            src, dst = scr.at[i, stage % 2], scr.at[i, 1 - stage % 2]
            if stage == 0:
                _to_slabs(src, val)
            ratio, rows = d // prev, tm // d
            for r in range(prev):
                for a in range(ratio):
                    part = _rows_strided(src, r * (tm // prev) + a, rows, ratio)
                    o_ref[r + prev * a] = part.astype(BF16)
                    if n_d + 1 < len(dilations):
                        _to_slabs(dst.at[:, pl.ds((r + prev * a) * rows, rows)], part)
            prev, stage = d, stage + 1

    pc_ref[...] = proj(0, 3 * wmix).astype(BF16)
    pf_ref[...] = proj(3 * wmix, 2 * wmix).astype(BF16)


def _in_proj(x, g, w, rope_tabs, batch, seq, tm, dilations, group):
    n, d = x.shape
    nin = w.shape[1]
    wmix = nin // 9
    per_seq = seq // tm
    row = lambda i: (i, 0)
    tab = lambda i: (i % per_seq, 0)
    flat = [(n, 3 * wmix), (n, 2 * wmix)]
    flat_blocks = [(tm, 3 * wmix), (tm, 2 * wmix)]
    uw = SSM_CHUNK * wmix
    u_shape = (batch // group, seq // SSM_CHUNK, group * uw)
    u_spec = pl.BlockSpec((None, tm // SSM_CHUNK, uw),
                          lambda i: (i // per_seq // group, i % per_seq, i // per_seq % group))
    res_shapes = [(batch, dl, seq // dl, wmix) for dl in dilations] * 3
    res_specs = [pl.BlockSpec((None, dl, tm // dl, wmix),
                              lambda i: (i // per_seq, 0, i % per_seq, 0)) for dl in dilations] * 3
    outs = pl.pallas_call(
        functools.partial(_in_proj_body, wmix=wmix, dilations=dilations),
        grid=(n // tm,),
        in_specs=[pl.BlockSpec((tm, d), row), _resident((1, d)), _resident((d, nin))]
        + [pl.BlockSpec((tm, wmix), tab)] * 4,
        out_specs=[pl.BlockSpec(b, row) for b in flat_blocks] + [u_spec] + res_specs,
        out_shape=[jax.ShapeDtypeStruct(s, BF16) for s in flat + [u_shape] + res_shapes],
        scratch_shapes=[pltpu.VMEM((4, 2, wmix // LANES, tm, LANES), F32)],
        compiler_params=_cparams("parallel"),
        name="in_proj",
    )(x, g.reshape(1, d), w, *rope_tabs)
    outs = list(outs)
    outs[2] = outs[2].reshape(batch // group, seq // SSM_CHUNK * group, uw)
    return outs


CONV_ROWS = 32
CFM_HALO = 32
SHORT_HALO = 16


def _conv_body(pc_ref, pch_ref, pf_ref, pfh_ref, w3_ref, wdw_ref, bdw_ref, lng_ref, lnb_ref,
               gn_ref, o_ref, zb3, zs, *, wmix):
    t_rows = pc_ref.shape[0]
    keep = (pl.program_id(1) > 0).astype(F32)

    pc = pc_ref[...].astype(F32)
    ch = pc[:, 2 * wmix:] * pc[:, :wmix]
    hal = pch_ref[...].astype(F32)
    zb3[0:SHORT_HALO, :] = hal[:, 2 * wmix:] * hal[:, :wmix] * keep
    zb3[SHORT_HALO:, :] = ch
    conv = w3_ref[SHORT_K - 1:SHORT_K, :] * ch
    for k in range(SHORT_K - 1):
        off = SHORT_HALO - (SHORT_K - 1) + k
        conv = conv + w3_ref[k:k + 1, :] * zb3[off:off + t_rows, :]
    y_conv = pc[:, wmix:2 * wmix] * conv
    o_ref[:, :wmix] = _rms(y_conv, gn_ref[:, :wmix]).astype(BF16)

    pf = pf_ref[...].astype(F32)
    ph = pfh_ref[...].astype(F32)
    zs[0, 0:CFM_HALO, :] = ph[:, :wmix] * jax.nn.sigmoid(ph[:, wmix:]) * keep
    zs[0, CFM_HALO:, :] = pf[:, :wmix] * jax.nn.sigmoid(pf[:, wmix:])
    span = t_rows + CFM_HALO - 8
    for b in range(1, 8):
        zs[b, 0:span, :] = zs[0, b:b + span, :]

    base = CFM_HALO - (CFM_K - 1)

    def rows(c, carry):
        r0 = pl.multiple_of(c * CONV_ROWS, CONV_ROWS)
        accs = [None] * 4
        for k in range(CFM_K):
            a, b = divmod(base + k, 8)
            taps = zs[b, pl.ds(r0 + 8 * a, CONV_ROWS), :].reshape(CONV_ROWS // 8, 8, wmix)
            term = wdw_ref[k] * taps
            accs[k % 4] = term if accs[k % 4] is None else accs[k % 4] + term
        acc = (accs[0] + accs[1]) + (accs[2] + accs[3])
        zb3[pl.ds(r0, CONV_ROWS), :] = acc.reshape(CONV_ROWS, wmix) + bdw_ref[...]
        return carry

    lax.fori_loop(0, t_rows // CONV_ROWS, rows, 0, unroll=2)

    acc = zb3[0:t_rows, :]
    mu = jnp.mean(acc, axis=-1, keepdims=True)
    xc = acc - mu
    var = jnp.mean(xc * xc, axis=-1, keepdims=True)
    y = xc * lax.rsqrt(var + EPS) * lng_ref[...] + lnb_ref[...]
    y = y * jax.nn.sigmoid(y)
    o_ref[:, wmix:] = _rms(y, gn_ref[:, wmix:]).astype(BF16)


def _conv_mixers(pc, pf, w3, wdw, bdw, lng, lnb, gn, batch, seq, tt):
    n = pc.shape[0]
    wmix = pf.shape[1] // 2
    per_seq = seq // tt
    row = lambda b, i: (b * per_seq + i, 0)

    def halo(rows_):
        return lambda b, i: (jnp.maximum((b * per_seq + i) * (tt // rows_) - 1, 0), 0)

    vec = lambda a: a.reshape(1, -1)
    return pl.pallas_call(
        functools.partial(_conv_body, wmix=wmix),
        grid=(batch, per_seq),
        in_specs=[pl.BlockSpec((tt, 3 * wmix), row),
                  pl.BlockSpec((SHORT_HALO, 3 * wmix), halo(SHORT_HALO)),
                  pl.BlockSpec((tt, 2 * wmix), row),
                  pl.BlockSpec((CFM_HALO, 2 * wmix), halo(CFM_HALO)),
                  _resident((SHORT_K, wmix)), _resident((CFM_K, 8, wmix)),
                  _resident((1, wmix)), _resident((1, wmix)), _resident((1, wmix)),
                  _resident((1, 2 * wmix))],
        out_specs=pl.BlockSpec((tt, 2 * wmix), row),
        out_shape=jax.ShapeDtypeStruct((n, 2 * wmix), BF16),
        scratch_shapes=[pltpu.VMEM((tt + SHORT_HALO, wmix), F32),
                        pltpu.VMEM((8, tt + CFM_HALO, wmix), F32)],
        compiler_params=_cparams("parallel", "arbitrary"),
        name="conv_mixers",
    )(pc, pc, pf, pf, w3, jnp.broadcast_to(wdw[:, None, :], (CFM_K, 8, wmix)), vec(bdw), vec(lng),
      vec(lnb), vec(gn))


def _cmul(ar, ai, br, bi):
    return ar * br - ai * bi, ar * bi + ai * br


def _ssm_mats(a_re, a_im, log_dt, b_re, b_im, c_re, c_im):
    hi = lax.Precision.HIGHEST
    ng, ns = a_re.shape
    nh = b_re.shape[-1]
    tc = SSM_CHUNK
    dt = jnp.exp(log_dt)[:, None]
    mag = jnp.exp(a_re * dt)
    lr, li = mag * jnp.cos(a_im * dt), mag * jnp.sin(a_im * dt)
    den = a_re * a_re + a_im * a_im
    nr, ni = lr - 1.0, li
    fr = (nr * a_re + ni * a_im) / den
    fi = (ni * a_re - nr * a_im) / den
    bbr = fr[..., None] * b_re - fi[..., None] * b_im
    bbi = fr[..., None] * b_im + fi[..., None] * b_re

    pr, pi = [jnp.ones_like(lr)], [jnp.zeros_like(li)]
    for _ in range(tc):
        r, i = _cmul(pr[-1], pi[-1], lr, li)
        pr.append(r)
        pi.append(i)
    pwr, pwi = jnp.stack(pr), jnp.stack(pi)

    lbr, lbi = _cmul(pwr[:tc, :, :, None], pwi[:tc, :, :, None], bbr[None], bbi[None])
    kd = (jnp.einsum('ghp,dgpk->dghk', c_re, lbr, precision=hi)
          - jnp.einsum('ghp,dgpk->dghk', c_im, lbi, precision=hi))
    tau = jnp.arange(tc)[:, None]
    tt = jnp.arange(tc)[None, :]
    lag = tt - tau
    kfull = jnp.where((lag >= 0)[:, :, None, None, None], kd[jnp.clip(lag, 0, tc - 1)], 0.0)
    m_full = kfull.transpose(0, 2, 4, 1, 3)
    w_r = lbr[::-1].transpose(0, 1, 3, 2)
    w_i = lbi[::-1].transpose(0, 1, 3, 2)
    clr, cli = _cmul(c_re[None], c_im[None], pwr[1:, :, None, :], pwi[1:, :, None, :])
    g_r = clr.transpose(1, 3, 0, 2)
    g_i = -cli.transpose(1, 3, 0, 2)

    gh = ng // 2
    eye = jnp.eye(gh, dtype=F32)
    ms, ws, gs, aps = [], [], [], []
    for j in range(2):
        sl = slice(j * gh, (j + 1) * gh)
        mj = m_full[:, sl][:, :, :, :, None, :] * eye[None, :, None, None, :, None]
        ms.append(mj.reshape(tc * gh * nh, tc * gh * nh))
        wj = jnp.stack([w_r[:, sl], w_i[:, sl]], axis=3)
        wj = wj[:, :, :, :, None, :] * eye[None, :, None, None, :, None]
        ws.append(wj.reshape(tc * gh * nh, 2 * gh * ns))
        gj = jnp.stack([g_r[sl], g_i[sl]], axis=0)
        gj = gj[:, :, :, :, None, :] * eye[None, :, None, None, :, None]
        gs.append(gj.reshape(2 * gh * ns, tc * gh * nh))
        aps.append(jnp.concatenate([pwr[tc, sl].reshape(1, -1), pwi[tc, sl].reshape(1, -1)], axis=1))
    return (jnp.stack(ms).astype(BF16), jnp.stack(ws).astype(BF16), jnp.stack(gs).astype(BF16),
            jnp.stack(aps))


def _ssm_body(u_ref, m_ref, w_ref, g_ref, a_ref, d_ref, wglu_ref, gn_ref, o_ref, state, zbuf, sbuf,
              zrow, ysc, tok, *, half, group):
    rows_ = u_ref.shape[0]
    steps = rows_ // group
    ch = 2 * half
    nst = a_ref.shape[2] // 2

    @pl.when(pl.program_id(1) == 0)
    def _():
        state[...] = jnp.zeros(state.shape, F32)

    lanes = [[slice(t * ch + j * half, t * ch + (j + 1) * half) for t in range(SSM_CHUNK)]
             for j in range(2)]
    lhs = [jnp.concatenate([u_ref[:, s] for s in lanes[j]], axis=1) for j in range(2)]
    for j in range(2):
        zbuf[j] = jnp.dot(lhs[j], w_ref[j], preferred_element_type=F32)

    lam = [(jnp.broadcast_to(a_ref[j, :, :nst], (group, nst)),
            jnp.broadcast_to(a_ref[j, :, nst:], (group, nst))) for j in range(2)]

    def chunk(r, carry):
        r0 = pl.multiple_of(r * group, group)
        out = []
        for j in range(2):
            sr, si = carry[2 * j], carry[2 * j + 1]
            sbuf[j, pl.ds(r0, group), :] = jnp.concatenate([sr, si], axis=1)
            z = zbuf[j, pl.ds(r0, group), :]
            ar, ai = lam[j]
            out += [ar * sr - ai * si + z[:, :nst], ar * si + ai * sr + z[:, nst:]]
        return tuple(out)

    final = lax.fori_loop(0, steps, chunk, tuple(state[i] for i in range(4)), unroll=8)
    for i in range(4):
        state[i] = final[i]

    for j in range(2):
        y = (jnp.dot(lhs[j], m_ref[j], preferred_element_type=F32)
             + jnp.dot(sbuf[j].astype(BF16), g_ref[j], preferred_element_type=F32))
        for t, s in enumerate(lanes[j]):
            yy = y[:, t * half:(t + 1) * half] + d_ref[:, s] * u_ref[:, s].astype(F32)
            zrow[:, s] = jax.nn.gelu(yy).astype(BF16)

    for t in range(SSM_CHUNK):
        gl = jnp.dot(zrow[:, t * ch:(t + 1) * ch], wglu_ref[...], preferred_element_type=F32)
        _to_slabs(ysc, _rms(gl[:, :ch] * jax.nn.sigmoid(gl[:, ch:]), gn_ref[...]))
        for b in range(group):
            for s in range(ysc.shape[0]):
                tok[b, s, pl.ds(t, steps, stride=SSM_CHUNK), :] = (
                    ysc[s, pl.ds(b, steps, stride=group), :])
    for b in range(group):
        o_ref[b] = jnp.concatenate([tok[b, s] for s in range(tok.shape[1])], axis=1).astype(BF16)


def _ssm_mixer(u_rows, mats, d_skip, wglu, gn, seq, tile_rows):
    n_groups, _, width = u_rows.shape
    ch = width // SSM_CHUNK
    m, w, g, a = mats
    group = u_rows.shape[1] * SSM_CHUNK // seq
    rows_ = tile_rows * group
    d_row = jnp.tile(d_skip, SSM_CHUNK).reshape(1, width)
    nst = a.shape[2] // 2
    out = pl.pallas_call(
        functools.partial(_ssm_body, half=ch // 2, group=group),
        grid=(n_groups, seq // SSM_CHUNK // tile_rows),
        in_specs=[pl.BlockSpec((None, rows_, width), lambda b, t: (b, t, 0)),
                  _resident(m.shape), _resident(w.shape), _resident(g.shape),
                  _resident(a.shape), _resident((1, width)), _resident(wglu.shape),
                  _resident((1, ch))],
        out_specs=pl.BlockSpec((group, tile_rows * SSM_CHUNK, ch), lambda b, t: (b, t, 0)),
        out_shape=jax.ShapeDtypeStruct((n_groups * group, seq, ch), BF16),
        scratch_shapes=[pltpu.VMEM((4, group, nst), F32),
                        pltpu.VMEM((2, rows_, 2 * nst), F32), pltpu.VMEM((2, rows_, 2 * nst), F32),
                        pltpu.VMEM((rows_, width), BF16),
                        pltpu.VMEM((ch // LANES, rows_, LANES), F32),
                        pltpu.VMEM((group, ch // LANES, tile_rows * SSM_CHUNK, LANES), F32)],
        compiler_params=_cparams("parallel", "arbitrary"),
        name="ssm_mixer",
    )(u_rows, m, w, g, a, d_row, wglu, gn.reshape(1, ch))
    return out.reshape(n_groups * group * seq, ch)


def _attn_body(q_ref, k_ref, v_ref, o_ref, st_ref, sbuf, pbuf, *, n_heads, qb):
    blk = ATT_BLK
    wd = q_ref.shape[1]
    nblk = q_ref.shape[0] // blk

    qi = lax.broadcasted_iota(jnp.int32, (blk, 2 * blk), 0)
    kj = lax.broadcasted_iota(jnp.int32, (blk, 2 * blk), 1)
    rel = blk + qi - kj
    bias_any = jnp.where((rel >= 0) & (rel <= blk), 0.0, NEG_BIG)
    bias_first = jnp.where(kj <= qi, 0.0, NEG_BIG)
    head_of_lane = lax.broadcasted_iota(jnp.int32, (1, wd), 1) // HEAD_DIM
    st_lane = lax.broadcasted_iota(jnp.int32, (1, st_ref.shape[1]), 1)

    def rows_of(f):
        return f * blk if isinstance(f, int) else pl.multiple_of(f * blk, blk)

    def window_of(f):
        first = (f % qb) == 0
        if isinstance(f, int):
            return (f if first else f - 1) * blk, first
        return pl.multiple_of((f - jnp.where(first, 0, 1)) * blk, blk), first

    def scores(f, slot):
        q = q_ref[pl.ds(rows_of(f), blk), :]
        qs = jnp.concatenate([jnp.where(head_of_lane == h, q, jnp.zeros_like(q))
                              for h in range(n_heads)], axis=0)
        sbuf[slot] = lax.dot_general(qs, k_ref[pl.ds(window_of(f)[0], 2 * blk), :],
                                     (((1,), (1,)), ((), ())), preferred_element_type=F32)

    def softmax(f, slot):
        first = window_of(f)[1]
        bias = (bias_first if first else bias_any) if isinstance(f, int) else jnp.where(
            first, bias_first, bias_any)
        stats = jnp.zeros((blk, st_ref.shape[1]), F32)
        for h in range(n_heads):
            s = sbuf[slot, h * blk:(h + 1) * blk, :] + bias
            m = jnp.max(s, axis=-1, keepdims=True)
            p = jnp.exp(s - m)
            pbuf[slot, h * blk:(h + 1) * blk, :] = p.astype(BF16)
            stats = jnp.where(st_lane == h, m, stats)
            stats = jnp.where(st_lane == n_heads + h, jnp.sum(p, axis=-1, keepdims=True), stats)
        st_ref[pl.ds(rows_of(f), blk), :] = stats

    def values(f, slot):
        o_all = jnp.dot(pbuf[slot], v_ref[pl.ds(window_of(f)[0], 2 * blk), :],
                        preferred_element_type=F32)
        o = o_all[0:blk]
        for h in range(1, n_heads):
            o = jnp.where(head_of_lane == h, o_all[h * blk:(h + 1) * blk], o)
        o_ref[pl.ds(rows_of(f), blk), :] = o.astype(BF16)

    def step(f, slot):
        scores(f, slot)
        softmax(f - 1, 1 - slot)
        values(f - 2, slot)

    scores(0, 0)
    scores(1, 1)
    softmax(0, 0)

    def pair(it, carry):
        step(2 * it + 2, 0)
        step(2 * it + 3, 1)
        return carry

    lax.fori_loop(0, (nblk - 2) // 2, pair, 0)
    softmax(nblk - 1, 1)
    values(nblk - 2, 0)
    values(nblk - 1, 1)


ST_LANES = LANES


def _attention(q, k, v):
    batch, dilation, rows_, wd = q.shape
    seq = dilation * rows_
    n_heads = wd // HEAD_DIM
    qb = rows_ // ATT_BLK
    assert qb >= 2 and qb & (qb - 1) == 0
    flat = lambda a: a.reshape(batch, seq, wd)
    spec = pl.BlockSpec((None, seq, wd), lambda b: (b, 0, 0))
    o, st = pl.pallas_call(
        functools.partial(_attn_body, n_heads=n_heads, qb=qb),
        grid=(batch,),
        in_specs=[spec, spec, spec],
        out_specs=[spec, pl.BlockSpec((None, seq, ST_LANES), lambda b: (b, 0, 0))],
        out_shape=[jax.ShapeDtypeStruct((batch, seq, wd), BF16),
                   jax.ShapeDtypeStruct((batch, seq, ST_LANES), F32)],
        scratch_shapes=[pltpu.VMEM((2, n_heads * ATT_BLK, 2 * ATT_BLK), F32),
                        pltpu.VMEM((2, n_heads * ATT_BLK, 2 * ATT_BLK), BF16)],
        compiler_params=_cparams("parallel"),
        name=f"attn_d{dilation}",
    )(flat(q), flat(k), flat(v))
    return (o.reshape(batch, dilation, rows_, wd), st.reshape(batch, dilation, rows_, ST_LANES))


def _token_order(block, d, bufs):
    slabs, tm, _ = bufs[0].shape
    stage = 0
    while d > 1:
        f = 4 if d % 4 == 0 else d
        nxt, rows, dst = d // f, tm // d, bufs[stage % 2]
        for r1 in range(nxt):
            for a in range(f):
                blk = block(r1 + nxt * a)
                for s in range(slabs):
                    dst[s, pl.ds(r1 * (tm // nxt) + a, rows, stride=f), :] = (
                        blk[:, s * LANES:(s + 1) * LANES])

        def block(r, dst=dst, size=tm // nxt):
            return jnp.concatenate([dst[s, pl.ds(r * size, size), :] for s in range(slabs)], axis=1)

        d, stage = nxt, stage + 1
    return block(0)


def _att_mix(o_refs, s_refs, gn_ref, oscr, sscr, dilations):
    tm, wd = oscr.shape[3], oscr.shape[2] * LANES
    n_heads = wd // HEAD_DIM
    outs, lses = [], []
    for p, d in enumerate(dilations):
        outs.append(_token_order(lambda r, p=p: o_refs[p][r].astype(F32), d,
                                 (oscr.at[p, 0], oscr.at[p, 1])))
        lses.append(_token_order(lambda r, p=p: s_refs[p][r], d, (sscr.at[p, 0], sscr.at[p, 1])))
    m = functools.reduce(jnp.maximum, lses)
    head_of_lane = lax.broadcasted_iota(jnp.int32, (1, wd), 1) // HEAD_DIM

    def widen(x, lane0):
        wide = jnp.zeros((tm, wd), F32)
        for h in range(n_heads):
            wide = jnp.where(head_of_lane == h, x[:, lane0 + h:lane0 + h + 1], wide)
        return wide

    num = jnp.zeros((tm, wd), F32)
    den = jnp.zeros((tm, wd), F32)
    for st, o in zip(lses, outs):
        e = widen(jnp.exp(st - m), 0)
        num = num + e * o
        den = den + e * widen(st, n_heads)
    return _rms(num / den, gn_ref[...]).astype(BF16)


FF_CHUNK = 256


def _out_ffn_body(x_ref, ycc_ref, ys_ref, *rest, final_norm, dilations):
    npat = len(dilations)
    o_refs, s_refs = rest[:npat], rest[npat:2 * npat]
    (gna_ref, wo_ref, gf_ref, wg_ref, wu_ref, wd_ref, gl_ref, o_ref, act, ya, oscr,
     sscr) = rest[2 * npat:]
    g = pl.program_id(0)

    @pl.when(g == 0)
    def _():
        ya[1] = jnp.zeros(ya.shape[1:], BF16)

    ya_prev = ya[(g + 1) % 2]
    ya[g % 2] = _att_mix(o_refs, s_refs, gna_ref, oscr, sscr, dilations)

    mixed = jnp.concatenate([ycc_ref[...], ys_ref[...], ya_prev], axis=1)
    x1 = x_ref[...] + jnp.dot(mixed, wo_ref[...], preferred_element_type=F32)
    o_ref[...] = x1
    h = _rms(x1, gf_ref[...]).astype(BF16)
    dff = wg_ref.shape[1]
    for c in range(dff // FF_CHUNK):
        sl = slice(c * FF_CHUNK, (c + 1) * FF_CHUNK)
        g = jnp.dot(h, wg_ref[:, sl], preferred_element_type=F32)
        u = jnp.dot(h, wu_ref[:, sl], preferred_element_type=F32)
        act[:, sl] = (g * jax.nn.sigmoid(g) * u).astype(BF16)
    x2 = o_ref[...] + jnp.dot(act[...], wd_ref[...], preferred_element_type=F32)
    if final_norm:
        x2 = _rms(x2, gl_ref[...])
    o_ref[...] = x2


def _out_ffn(x, ycc, ys, att_outs, att_stats, gna, wo, gf, wg, wu, wd, gl, final_norm, tm):
    n, d = x.shape
    dff = wg.shape[1]
    batch, _, _, wa = att_outs[0].shape
    seq = n // batch
    dilations = tuple(o.shape[1] for o in att_outs)
    per_seq = seq // tm
    tiles = n // tm
    row = lambda g: (jnp.maximum(g - 1, 0), 0)

    def res_spec(dl, width):
        def index(g):
            t = jnp.minimum(g, tiles - 1)
            return (t // per_seq, 0, t % per_seq, 0)
        return pl.BlockSpec((None, dl, tm // dl, width), index)

    return pl.pallas_call(
        functools.partial(_out_ffn_body, final_norm=final_norm, dilations=dilations),
        grid=(tiles + 1,),
        in_specs=[pl.BlockSpec((tm, d), row), pl.BlockSpec((tm, ycc.shape[1]), row),
                  pl.BlockSpec((tm, ys.shape[1]), row)]
        + [res_spec(dl, wa) for dl in dilations] + [res_spec(dl, ST_LANES) for dl in dilations]
        + [_resident((1, wa)), _resident(wo.shape), _resident((1, d)), _resident(wg.shape),
           _resident(wu.shape), _resident(wd.shape), _resident((1, d))],
        out_specs=pl.BlockSpec((tm, d), row),
        out_shape=jax.ShapeDtypeStruct((n, d), F32),
        scratch_shapes=[pltpu.VMEM((tm, dff), BF16), pltpu.VMEM((2, tm, wa), BF16),
                        pltpu.VMEM((len(dilations), 2, wa // LANES, tm, LANES), F32),
                        pltpu.VMEM((len(dilations), 2, ST_LANES // LANES, tm, LANES), F32)],
        compiler_params=_cparams("arbitrary"),
        name="out_ffn",
    )(x, ycc, ys, *att_outs, *att_stats, gna.reshape(1, wa), wo, gf.reshape(1, d), wg, wu, wd,
      gl.reshape(1, d))


def _rope_tables(seq, wd):
    half = HEAD_DIM // 2
    inv = ROPE_THETA ** (-jnp.arange(0, HEAD_DIM, 2, dtype=F32) / HEAD_DIM)
    ang = jnp.arange(seq, dtype=F32)[:, None] * inv[None, :]
    cos, sin = jnp.cos(ang), jnp.sin(ang)
    reps = wd // HEAD_DIM
    cos_t = jnp.tile(jnp.concatenate([cos, cos], axis=1), (1, reps))
    sin_t = jnp.tile(jnp.concatenate([-sin, sin], axis=1), (1, reps))
    scale = HEAD_DIM ** -0.5
    return cos_t * scale, sin_t * scale, cos_t, sin_t


def kernel(x, norm_mix_g, w_in, conv3_w, cfm_dw_w, cfm_dw_b, cfm_ln_g, cfm_ln_b, s5_a_re, s5_a_im, s5_log_dt, s5_b_re, s5_b_im, s5_c_re, s5_c_im, s5_d, s5_glu_w, grp_norm_g, w_out, norm_ffn_g, w_gate, w_up, w_down, final_norm_g):
    batch, seq, d_model = x.shape
    depth = w_in.shape[0]
    wmix = w_in.shape[2] // 9
    n = batch * seq
    assert wmix == 2 * 8 * SSM_CH and all(w // d == ATT_BLK for w, d in DILATED_CFG)
    assert seq % (DILATED_CFG[-1][1] * ATT_BLK) == 0 and seq % 512 == 0
    tm = 512
    group = math.gcd(batch, 8)
    rope_tabs = _rope_tables(seq, wmix)

    dilations = tuple(dil for _, dil in DILATED_CFG)
    npat = len(dilations)

    xf = x.reshape(n, d_model)
    for layer in range(depth):
        pc, pf, u, *qkv = _in_proj(xf, norm_mix_g[layer], w_in[layer].astype(BF16), rope_tabs,
                                   batch, seq, tm, dilations, group)
        gn = grp_norm_g[layer]
        ycc = _conv_mixers(pc, pf, conv3_w[layer], cfm_dw_w[layer], cfm_dw_b[layer],
                           cfm_ln_g[layer], cfm_ln_b[layer], gn[:2 * wmix], batch, seq, 512)
        mats = _ssm_mats(s5_a_re[layer], s5_a_im[layer], s5_log_dt[layer], s5_b_re[layer],
                         s5_b_im[layer], s5_c_re[layer], s5_c_im[layer])
        ys = _ssm_mixer(u, mats, s5_d[layer], s5_glu_w[layer].astype(BF16),
                        gn[2 * wmix:3 * wmix], seq, tm // SSM_CHUNK)
        outs, stats = zip(*[_attention(qkv[p], qkv[npat + p], qkv[2 * npat + p])
                            for p in range(npat)])
        xf = _out_ffn(xf, ycc, ys, outs, stats, gn[3 * wmix:], w_out[layer].astype(BF16),
                      norm_ffn_g[layer],
                      w_gate[layer].astype(BF16), w_up[layer].astype(BF16),
                      w_down[layer].astype(BF16), final_norm_g, layer == depth - 1, tm)
    return xf.reshape(batch, seq, d_model)
```

```python
import functools
import math

import jax
import jax.numpy as jnp
from jax import lax
from jax.experimental import pallas as pl
from jax.experimental.pallas import tpu as pltpu

F32 = jnp.float32
BF16 = jnp.bfloat16

EPS = 1e-6
HEAD_DIM = 64
SHORT_K = 3
CFM_K = 31
SSM_CH = 16
SSM_STATE = 64
SSM_CHUNK = 8
DILATED_CFG = ((128, 1), (512, 4), (2048, 16))
ROPE_THETA = 10000.0
ATT_BLK = 128
NEG_BIG = -1e30
VMEM_LIMIT = 56 * 1024 * 1024


def _cparams(*sem):
    return pltpu.CompilerParams(dimension_semantics=sem, vmem_limit_bytes=VMEM_LIMIT)


def _rms(x, g):
    return x * lax.rsqrt(jnp.mean(x * x, axis=-1, keepdims=True) + EPS) * g


def _resident(shape):
    nd = len(shape)
    return pl.BlockSpec(shape, lambda *_: (0,) * nd, pipeline_mode=pl.Buffered(1))


LANES = 128


def _to_slabs(scr, val):
    for s in range(scr.shape[0]):
        scr[s] = val[:, s * LANES:(s + 1) * LANES]


def _rows_strided(scr, start, count, stride):
    return jnp.concatenate([scr[s, pl.ds(start, count, stride=stride), :]
                            for s in range(scr.shape[0])], axis=1)


def _in_proj_body(x_ref, g_ref, w_ref, cq_ref, sq_ref, ck_ref, sk_ref, pc_ref, pf_ref, u_ref,
                  *rest, wmix, dilations):
    qkv_refs, scr = rest[:-1], rest[-1]
    tm = x_ref.shape[0]
    h = _rms(x_ref[...], g_ref[...]).astype(BF16)

    def proj(lo, width):
        return jnp.dot(h, w_ref[:, lo:lo + width], preferred_element_type=F32)

    _to_slabs(scr.at[3, 0], proj(5 * wmix, wmix))
    for t in range(SSM_CHUNK):
        u_ref[:, t * wmix:(t + 1) * wmix] = _rows_strided(
            scr.at[3, 0], t, tm // SSM_CHUNK, SSM_CHUNK).astype(BF16)

    lane = lax.broadcasted_iota(jnp.int32, (1, wmix), 1)
    first_half = (lane % HEAD_DIM) < (HEAD_DIM // 2)

    def rope(t, c_ref, s_ref):
        partner = jnp.where(first_half, pltpu.roll(t, wmix - HEAD_DIM // 2, 1),
                            pltpu.roll(t, HEAD_DIM // 2, 1))
        return t * c_ref[...] + partner * s_ref[...]

    vals = (rope(proj(6 * wmix, wmix), cq_ref, sq_ref), rope(proj(7 * wmix, wmix), ck_ref, sk_ref),
            proj(8 * wmix, wmix))
    for i, val in enumerate(vals):
        outs = qkv_refs[i * len(dilations):(i + 1) * len(dilations)]
        prev, stage = 1, 0
        for n_d, (d, o_ref) in enumerate(zip(dilations, outs)):
            if d == 1:
                o_ref[0] = val.astype(BF16)
                continue
            src, dst = scr.at[i, stage % 2], scr.at[i, 1 - stage % 2]
            if stage == 0:
                _to_slabs(src, val)
            ratio, rows = d // prev, tm // d
            for r in range(prev):
                for a in range(ratio):
                    part = _rows_strided(src, r * (tm // prev) + a, rows, ratio)
                    o_ref[r + prev * a] = part.astype(BF16)
                    if n_d + 1 < len(dilations):
                        _to_slabs(dst.at[:, pl.ds((r + prev * a) * rows, rows)], part)
            prev, stage = d, stage + 1

    pc_ref[...] = proj(0, 3 * wmix).astype(BF16)
    pf_ref[...] = proj(3 * wmix, 2 * wmix).astype(BF16)


def _in_proj(x, g, w, rope_tabs, batch, seq, tm, dilations, group):
    n, d = x.shape
    nin = w.shape[1]
    wmix = nin // 9
    per_seq = seq // tm
    row = lambda i: (i, 0)
    tab = lambda i: (i % per_seq, 0)
    flat = [(n, 3 * wmix), (n, 2 * wmix)]
    flat_blocks = [(tm, 3 * wmix), (tm, 2 * wmix)]
    uw = SSM_CHUNK * wmix
    u_shape = (batch // group, seq // SSM_CHUNK, group * uw)
    u_spec = pl.BlockSpec((None, tm // SSM_CHUNK, uw),
                          lambda i: (i // per_seq // group, i % per_seq, i // per_seq % group))
    res_shapes = [(batch, dl, seq // dl, wmix) for dl in dilations] * 3
    res_specs = [pl.BlockSpec((None, dl, tm // dl, wmix),
                              lambda i: (i // per_seq, 0, i % per_seq, 0)) for dl in dilations] * 3
    outs = pl.pallas_call(
        functools.partial(_in_proj_body, wmix=wmix, dilations=dilations),
        grid=(n // tm,),
        in_specs=[pl.BlockSpec((tm, d), row), _resident((1, d)), _resident((d, nin))]
        + [pl.BlockSpec((tm, wmix), tab)] * 4,
        out_specs=[pl.BlockSpec(b, row) for b in flat_blocks] + [u_spec] + res_specs,
        out_shape=[jax.ShapeDtypeStruct(s, BF16) for s in flat + [u_shape] + res_shapes],
        scratch_shapes=[pltpu.VMEM((4, 2, wmix // LANES, tm, LANES), F32)],
        compiler_params=_cparams("parallel"),
        name="in_proj",
    )(x, g.reshape(1, d), w, *rope_tabs)
    return outs


CONV_ROWS = 32
CFM_HALO = 32
SHORT_HALO = 16


def _conv_body(pc_ref, pch_ref, pf_ref, pfh_ref, w3_ref, wdw_ref, bdw_ref, lng_ref, lnb_ref,
               gn_ref, o_ref, zb3, zs, *, wmix):
    t_rows = pc_ref.shape[0]
    keep = (pl.program_id(1) > 0).astype(F32)

    pc = pc_ref[...].astype(F32)
    ch = pc[:, 2 * wmix:] * pc[:, :wmix]
    hal = pch_ref[...].astype(F32)
    zb3[0:SHORT_HALO, :] = hal[:, 2 * wmix:] * hal[:, :wmix] * keep
    zb3[SHORT_HALO:, :] = ch
    conv = w3_ref[SHORT_K - 1:SHORT_K, :] * ch
    for k in range(SHORT_K - 1):
        off = SHORT_HALO - (SHORT_K - 1) + k
        conv = conv + w3_ref[k:k + 1, :] * zb3[off:off + t_rows, :]
    y_conv = pc[:, wmix:2 * wmix] * conv
    o_ref[:, :wmix] = _rms(y_conv, gn_ref[:, :wmix]).astype(BF16)

    pf = pf_ref[...].astype(F32)
    ph = pfh_ref[...].astype(F32)
    zs[0, 0:CFM_HALO, :] = ph[:, :wmix] * jax.nn.sigmoid(ph[:, wmix:]) * keep
    zs[0, CFM_HALO:, :] = pf[:, :wmix] * jax.nn.sigmoid(pf[:, wmix:])
    span = t_rows + CFM_HALO - 8
    for b in range(1, 8):
        zs[b, 0:span, :] = zs[0, b:b + span, :]

    base = CFM_HALO - (CFM_K - 1)

    def rows(c, carry):
        r0 = pl.multiple_of(c * CONV_ROWS, CONV_ROWS)
        accs = [None] * 4
        for b in range(8):
            ks = [k for k in range(CFM_K) if (base + k) % 8 == b]
            hi = max((base + k) // 8 for k in ks)
            win = zs[b, pl.ds(r0, CONV_ROWS + 8 * hi), :].reshape(CONV_ROWS // 8 + hi, 8, wmix)
            for k in ks:
                a = (base + k) // 8
                term = wdw_ref[k] * win[a:a + CONV_ROWS // 8]
                accs[k % 4] = term if accs[k % 4] is None else accs[k % 4] + term
        acc = (accs[0] + accs[1]) + (accs[2] + accs[3])
        zb3[pl.ds(r0, CONV_ROWS), :] = acc.reshape(CONV_ROWS, wmix) + bdw_ref[...]
        return carry

    lax.fori_loop(0, t_rows // CONV_ROWS, rows, 0, unroll=2)

    acc = zb3[0:t_rows, :]
    mu = jnp.mean(acc, axis=-1, keepdims=True)
    xc = acc - mu
    var = jnp.mean(xc * xc, axis=-1, keepdims=True)
    y = xc * lax.rsqrt(var + EPS) * lng_ref[...] + lnb_ref[...]
    y = y * jax.nn.sigmoid(y)
    o_ref[:, wmix:] = _rms(y, gn_ref[:, wmix:]).astype(BF16)


def _conv_mixers(pc, pf, w3, wdw, bdw, lng, lnb, gn, batch, seq, tt):
    n = pc.shape[0]
    wmix = pf.shape[1] // 2
    per_seq = seq // tt
    row = lambda b, i: (b * per_seq + i, 0)

    def halo(rows_):
        return lambda b, i: (jnp.maximum((b * per_seq + i) * (tt // rows_) - 1, 0), 0)

    vec = lambda a: a.reshape(1, -1)
    return pl.pallas_call(
        functools.partial(_conv_body, wmix=wmix),
        grid=(batch, per_seq),
        in_specs=[pl.BlockSpec((tt, 3 * wmix), row),
                  pl.BlockSpec((SHORT_HALO, 3 * wmix), halo(SHORT_HALO)),
                  pl.BlockSpec((tt, 2 * wmix), row),
                  pl.BlockSpec((CFM_HALO, 2 * wmix), halo(CFM_HALO)),
                  _resident((SHORT_K, wmix)), _resident((CFM_K, 8, wmix)),
                  _resident((1, wmix)), _resident((1, wmix)), _resident((1, wmix)),
                  _resident((1, 2 * wmix))],
        out_specs=pl.BlockSpec((tt, 2 * wmix), row),
        out_shape=jax.ShapeDtypeStruct((n, 2 * wmix), BF16),
        scratch_shapes=[pltpu.VMEM((tt + SHORT_HALO, wmix), F32),
                        pltpu.VMEM((8, tt + CFM_HALO, wmix), F32)],
        compiler_params=_cparams("parallel", "arbitrary"),
        name="conv_mixers",
    )(pc, pc, pf, pf, w3, jnp.broadcast_to(wdw[:, None, :], (CFM_K, 8, wmix)), vec(bdw), vec(lng),
      vec(lnb), vec(gn))


def _cmul(ar, ai, br, bi):
    return ar * br - ai * bi, ar * bi + ai * br


def _ssm_mats(a_re, a_im, log_dt, b_re, b_im, c_re, c_im):
    hi = lax.Precision.HIGHEST
    ng, ns = a_re.shape
    nh = b_re.shape[-1]
    tc = SSM_CHUNK
    dt = jnp.exp(log_dt)[:, None]
    mag = jnp.exp(a_re * dt)
    lr, li = mag * jnp.cos(a_im * dt), mag * jnp.sin(a_im * dt)
    den = a_re * a_re + a_im * a_im
    nr, ni = lr - 1.0, li
    fr = (nr * a_re + ni * a_im) / den
    fi = (ni * a_re - nr * a_im) / den
    bbr = fr[..., None] * b_re - fi[..., None] * b_im
    bbi = fr[..., None] * b_im + fi[..., None] * b_re

    pr, pi = [jnp.ones_like(lr)], [jnp.zeros_like(li)]
    for _ in range(tc):
        r, i = _cmul(pr[-1], pi[-1], lr, li)
        pr.append(r)
        pi.append(i)
    pwr, pwi = jnp.stack(pr), jnp.stack(pi)

    lbr, lbi = _cmul(pwr[:tc, :, :, None], pwi[:tc, :, :, None], bbr[None], bbi[None])
    kd = (jnp.einsum('ghp,dgpk->dghk', c_re, lbr, precision=hi)
          - jnp.einsum('ghp,dgpk->dghk', c_im, lbi, precision=hi))
    tau = jnp.arange(tc)[:, None]
    tt = jnp.arange(tc)[None, :]
    lag = tt - tau
    kfull = jnp.where((lag >= 0)[:, :, None, None, None], kd[jnp.clip(lag, 0, tc - 1)], 0.0)
    m_full = kfull.transpose(0, 2, 4, 1, 3)
    w_r = lbr[::-1].transpose(0, 1, 3, 2)
    w_i = lbi[::-1].transpose(0, 1, 3, 2)
    clr, cli = _cmul(c_re[None], c_im[None], pwr[1:, :, None, :], pwi[1:, :, None, :])
    g_r = clr.transpose(1, 3, 0, 2)
    g_i = -cli.transpose(1, 3, 0, 2)

    gh = ng // 2
    eye = jnp.eye(gh, dtype=F32)
    ms, ws, gs, aps = [], [], [], []
    for j in range(2):
        sl = slice(j * gh, (j + 1) * gh)
        mj = m_full[:, sl][:, :, :, :, None, :] * eye[None, :, None, None, :, None]
        ms.append(mj.astype(BF16).reshape(tc * gh * nh, tc * gh * nh))
        wj = jnp.stack([w_r[:, sl], w_i[:, sl]], axis=3)
        wj = wj[:, :, :, :, None, :] * eye[None, :, None, None, :, None]
        ws.append(wj.astype(BF16).reshape(tc * gh * nh, 2 * gh * ns))
        gj = jnp.stack([g_r[sl], g_i[sl]], axis=0)
        gj = gj[:, :, :, :, None, :] * eye[None, :, None, None, :, None]
        gs.append(gj.astype(BF16).reshape(2 * gh * ns, tc * gh * nh))
        aps.append(jnp.concatenate([pwr[tc, sl].reshape(1, -1), pwi[tc, sl].reshape(1, -1)], axis=1))
    return jnp.stack(ms), jnp.stack(ws), jnp.stack(gs), jnp.stack(aps)


def _ssm_body(u_ref, m_ref, w_ref, g_ref, a_ref, d_ref, wglu_ref, gn_ref, o_ref, state, zbuf, sbuf,
              zrow, ysc, uscr, tok, *, half, group):
    steps = u_ref.shape[0]
    width = u_ref.shape[1] // group
    ch = 2 * half
    nst = a_ref.shape[2] // 2
    assert half == LANES

    for b in range(group):
        part = u_ref[:, b * width:(b + 1) * width].astype(F32)
        for s in range(width // LANES):
            uscr[s, pl.ds(b, steps, stride=group), :] = part[:, s * LANES:(s + 1) * LANES]

    @pl.when(pl.program_id(1) == 0)
    def _():
        state[...] = jnp.zeros(state.shape, F32)

    lanes = [[slice(t * ch + j * half, t * ch + (j + 1) * half) for t in range(SSM_CHUNK)]
             for j in range(2)]
    lhs = [jnp.concatenate([uscr[s.start // LANES] for s in lanes[j]], axis=1).astype(BF16)
           for j in range(2)]
    for j in range(2):
        zbuf[j] = jnp.dot(lhs[j], w_ref[j], preferred_element_type=F32)

    lam = [(jnp.broadcast_to(a_ref[j, :, :nst], (group, nst)),
            jnp.broadcast_to(a_ref[j, :, nst:], (group, nst))) for j in range(2)]

    def chunk(r, carry):
        r0 = pl.multiple_of(r * group, group)
        out = []
        for j in range(2):
            sr, si = carry[2 * j], carry[2 * j + 1]
            sbuf[j, pl.ds(r0, group), :] = jnp.concatenate([sr, si], axis=1)
            z = zbuf[j, pl.ds(r0, group), :]
            ar, ai = lam[j]
            out += [ar * sr - ai * si + z[:, :nst], ar * si + ai * sr + z[:, nst:]]
        return tuple(out)

    final = lax.fori_loop(0, steps, chunk, tuple(state[i] for i in range(4)), unroll=8)
    for i in range(4):
        state[i] = final[i]

    for j in range(2):
        y = (jnp.dot(lhs[j], m_ref[j], preferred_element_type=F32)
             + jnp.dot(sbuf[j].astype(BF16), g_ref[j], preferred_element_type=F32))
        for t, s in enumerate(lanes[j]):
            yy = y[:, t * half:(t + 1) * half] + d_ref[:, s] * uscr[s.start // LANES]
            zrow[:, s] = jax.nn.gelu(yy).astype(BF16)

    for t in range(SSM_CHUNK):
        gl = jnp.dot(zrow[:, t * ch:(t + 1) * ch], wglu_ref[...], preferred_element_type=F32)
        _to_slabs(ysc, _rms(gl[:, :ch] * jax.nn.sigmoid(gl[:, ch:]), gn_ref[...]))
        for b in range(group):
            for s in range(ysc.shape[0]):
                tok[b, s, pl.ds(t, steps, stride=SSM_CHUNK), :] = (
                    ysc[s, pl.ds(b, steps, stride=group), :])
    for b in range(group):
        o_ref[b] = jnp.concatenate([tok[b, s] for s in range(tok.shape[1])], axis=1).astype(BF16)


def _ssm_mixer(u_rows, mats, d_skip, wglu, gn, seq, tile_rows):
    n_groups = u_rows.shape[0]
    width = SSM_CHUNK * wglu.shape[0]
    ch = width // SSM_CHUNK
    m, w, g, a = mats
    group = u_rows.shape[2] // width
    rows_ = tile_rows * group
    d_row = jnp.tile(d_skip, SSM_CHUNK).reshape(1, width)
    nst = a.shape[2] // 2
    out = pl.pallas_call(
        functools.partial(_ssm_body, half=ch // 2, group=group),
        grid=(n_groups, seq // SSM_CHUNK // tile_rows),
        in_specs=[pl.BlockSpec((None, tile_rows, group * width), lambda b, t: (b, t, 0)),
                  _resident(m.shape), _resident(w.shape), _resident(g.shape),
                  _resident(a.shape), _resident((1, width)), _resident(wglu.shape),
                  _resident((1, ch))],
        out_specs=pl.BlockSpec((group, tile_rows * SSM_CHUNK, ch), lambda b, t: (b, t, 0)),
        out_shape=jax.ShapeDtypeStruct((n_groups * group, seq, ch), BF16),
        scratch_shapes=[pltpu.VMEM((4, group, nst), F32),
                        pltpu.VMEM((2, rows_, 2 * nst), F32), pltpu.VMEM((2, rows_, 2 * nst), F32),
                        pltpu.VMEM((rows_, width), BF16),
                        pltpu.VMEM((ch // LANES, rows_, LANES), F32),
                        pltpu.VMEM((width // LANES, rows_, LANES), F32),
                        pltpu.VMEM((group, ch // LANES, tile_rows * SSM_CHUNK, LANES), F32)],
        compiler_params=_cparams("parallel", "arbitrary"),
        name="ssm_mixer",
    )(u_rows, m, w, g, a, d_row, wglu, gn.reshape(1, ch))
    return out.reshape(n_groups * group * seq, ch)


def _attn_body(q_ref, k_ref, v_ref, o_ref, st_ref, sbuf, pbuf, *, n_heads, qb):
    blk = ATT_BLK
    wd = q_ref.shape[1]
    nblk = q_ref.shape[0] // blk

    qi = lax.broadcasted_iota(jnp.int32, (blk, 2 * blk), 0)
    kj = lax.broadcasted_iota(jnp.int32, (blk, 2 * blk), 1)
    rel = blk + qi - kj
    bias_any = jnp.where((rel >= 0) & (rel <= blk), 0.0, NEG_BIG)
    bias_first = jnp.where(kj <= qi, 0.0, NEG_BIG)
    head_of_lane = lax.broadcasted_iota(jnp.int32, (1, wd), 1) // HEAD_DIM
    st_lane = lax.broadcasted_iota(jnp.int32, (1, st_ref.shape[1]), 1)

    def rows_of(f):
        return f * blk if isinstance(f, int) else pl.multiple_of(f * blk, blk)

    def window_of(f):
        first = (f % qb) == 0
        if isinstance(f, int):
            return (f if first else f - 1) * blk, first
        return pl.multiple_of((f - jnp.where(first, 0, 1)) * blk, blk), first

    def scores(f, slot):
        q = q_ref[pl.ds(rows_of(f), blk), :]
        qs = jnp.concatenate([jnp.where(head_of_lane == h, q, jnp.zeros_like(q))
                              for h in range(n_heads)], axis=0)
        sbuf[slot] = lax.dot_general(qs, k_ref[pl.ds(window_of(f)[0], 2 * blk), :],
                                     (((1,), (1,)), ((), ())), preferred_element_type=F32)

    def softmax(f, slot):
        first = window_of(f)[1]
        bias = (bias_first if first else bias_any) if isinstance(f, int) else jnp.where(
            first, bias_first, bias_any)
        stats = jnp.zeros((blk, st_ref.shape[1]), F32)
        for h in range(n_heads):
            s = sbuf[slot, h * blk:(h + 1) * blk, :] + bias
            m = jnp.max(s, axis=-1, keepdims=True)
            p = jnp.exp(s - m)
            pbuf[slot, h * blk:(h + 1) * blk, :] = p.astype(BF16)
            stats = jnp.where(st_lane == h, m, stats)
            stats = jnp.where(st_lane == n_heads + h, jnp.sum(p, axis=-1, keepdims=True), stats)
        st_ref[pl.ds(rows_of(f), blk), :] = stats

    def values(f, slot):
        o_all = jnp.dot(pbuf[slot], v_ref[pl.ds(window_of(f)[0], 2 * blk), :],
                        preferred_element_type=F32)
        o = o_all[0:blk]
        for h in range(1, n_heads):
            o = jnp.where(head_of_lane == h, o_all[h * blk:(h + 1) * blk], o)
        o_ref[pl.ds(rows_of(f), blk), :] = o.astype(BF16)

    def step(f, slot):
        scores(f, slot)
        softmax(f - 1, 1 - slot)
        values(f - 2, slot)

    scores(0, 0)
    scores(1, 1)
    softmax(0, 0)

    def pair(it, carry):
        step(2 * it + 2, 0)
        step(2 * it + 3, 1)
        return carry

    lax.fori_loop(0, (nblk - 2) // 2, pair, 0)
    softmax(nblk - 1, 1)
    values(nblk - 2, 0)
    values(nblk - 1, 1)


ST_LANES = LANES


def _attention(q, k, v):
    batch, dilation, rows_, wd = q.shape
    seq = dilation * rows_
    n_heads = wd // HEAD_DIM
    qb = rows_ // ATT_BLK
    assert qb >= 2 and qb & (qb - 1) == 0
    flat = lambda a: a.reshape(batch, seq, wd)
    spec = pl.BlockSpec((None, seq, wd), lambda b: (b, 0, 0))
    o, st = pl.pallas_call(
        functools.partial(_attn_body, n_heads=n_heads, qb=qb),
        grid=(batch,),
        in_specs=[spec, spec, spec],
        out_specs=[spec, pl.BlockSpec((None, seq, ST_LANES), lambda b: (b, 0, 0))],
        out_shape=[jax.ShapeDtypeStruct((batch, seq, wd), BF16),
                   jax.ShapeDtypeStruct((batch, seq, ST_LANES), F32)],
        scratch_shapes=[pltpu.VMEM((2, n_heads * ATT_BLK, 2 * ATT_BLK), F32),
                        pltpu.VMEM((2, n_heads * ATT_BLK, 2 * ATT_BLK), BF16)],
        compiler_params=_cparams("parallel"),
        name=f"attn_d{dilation}",
    )(flat(q), flat(k), flat(v))
    return (o.reshape(batch, dilation, rows_, wd), st.reshape(batch, dilation, rows_, ST_LANES))


def _token_order(block, d, bufs):
    slabs, tm, _ = bufs[0].shape
    stage = 0
    while d > 1:
        f = 4 if d % 4 == 0 else d
        nxt, rows, dst = d // f, tm // d, bufs[stage % 2]
        for r1 in range(nxt):
            for a in range(f):
                blk = block(r1 + nxt * a)
                for s in range(slabs):
                    dst[s, pl.ds(r1 * (tm // nxt) + a, rows, stride=f), :] = (
                        blk[:, s * LANES:(s + 1) * LANES])

        def block(r, dst=dst, size=tm // nxt):
            return jnp.concatenate([dst[s, pl.ds(r * size, size), :] for s in range(slabs)], axis=1)

        d, stage = nxt, stage + 1
    return block(0)


def _att_mix(o_refs, s_refs, gn_ref, oscr, sscr, dilations):
    tm, wd = oscr.shape[3], oscr.shape[2] * LANES
    n_heads = wd // HEAD_DIM
    outs, lses = [], []
    for p, d in enumerate(dilations):
        outs.append(_token_order(lambda r, p=p: o_refs[p][r].astype(F32), d,
                                 (oscr.at[p, 0], oscr.at[p, 1])))
        lses.append(_token_order(lambda r, p=p: s_refs[p][r], d, (sscr.at[p, 0], sscr.at[p, 1])))
    m = functools.reduce(jnp.maximum, lses)
    head_of_lane = lax.broadcasted_iota(jnp.int32, (1, wd), 1) // HEAD_DIM

    def widen(x, lane0):
        wide = jnp.zeros((tm, wd), F32)
        for h in range(n_heads):
            wide = jnp.where(head_of_lane == h, x[:, lane0 + h:lane0 + h + 1], wide)
        return wide

    num = jnp.zeros((tm, wd), F32)
    den = jnp.zeros((tm, wd), F32)
    for st, o in zip(lses, outs):
        e = widen(jnp.exp(st - m), 0)
        num = num + e * o
        den = den + e * widen(st, n_heads)
    return _rms(num / den, gn_ref[...]).astype(BF16)


FF_CHUNK = 256


def _out_ffn_body(x_ref, ycc_ref, ys_ref, *rest, final_norm, dilations):
    npat = len(dilations)
    o_refs, s_refs = rest[:npat], rest[npat:2 * npat]
    (gna_ref, wo_ref, gf_ref, wg_ref, wu_ref, wd_ref, gl_ref, o_ref, act, ya, oscr,
     sscr) = rest[2 * npat:]
    g = pl.program_id(0)

    @pl.when(g == 0)
    def _():
        ya[1] = jnp.zeros(ya.shape[1:], BF16)

    ya_prev = ya[(g + 1) % 2]
    ya[g % 2] = _att_mix(o_refs, s_refs, gna_ref, oscr, sscr, dilations)

    mixed = jnp.concatenate([ycc_ref[...], ys_ref[...], ya_prev], axis=1)
    x1 = x_ref[...] + jnp.dot(mixed, wo_ref[...], preferred_element_type=F32)
    o_ref[...] = x1
    h = _rms(x1, gf_ref[...]).astype(BF16)
    dff = wg_ref.shape[1]
    for c in range(dff // FF_CHUNK):
        sl = slice(c * FF_CHUNK, (c + 1) * FF_CHUNK)
        g = jnp.dot(h, wg_ref[:, sl], preferred_element_type=F32)
        u = jnp.dot(h, wu_ref[:, sl], preferred_element_type=F32)
        act[:, sl] = (g * jax.nn.sigmoid(g) * u).astype(BF16)
    x2 = o_ref[...] + jnp.dot(act[...], wd_ref[...], preferred_element_type=F32)
    if final_norm:
        x2 = _rms(x2, gl_ref[...])
    o_ref[...] = x2


def _out_ffn(x, ycc, ys, att_outs, att_stats, gna, wo, gf, wg, wu, wd, gl, final_norm, tm):
    n, d = x.shape
    dff = wg.shape[1]
    batch, _, _, wa = att_outs[0].shape
    seq = n // batch
    dilations = tuple(o.shape[1] for o in att_outs)
    per_seq = seq // tm
    tiles = n // tm
    row = lambda g: (jnp.maximum(g - 1, 0), 0)

    def res_spec(dl, width):
        def index(g):
            t = jnp.minimum(g, tiles - 1)
            return (t // per_seq, 0, t % per_seq, 0)
        return pl.BlockSpec((None, dl, tm // dl, width), index)

    return pl.pallas_call(
        functools.partial(_out_ffn_body, final_norm=final_norm, dilations=dilations),
        grid=(tiles + 1,),
        in_specs=[pl.BlockSpec((tm, d), row), pl.BlockSpec((tm, ycc.shape[1]), row),
                  pl.BlockSpec((tm, ys.shape[1]), row)]
        + [res_spec(dl, wa) for dl in dilations] + [res_spec(dl, ST_LANES) for dl in dilations]
        + [_resident((1, wa)), _resident(wo.shape), _resident((1, d)), _resident(wg.shape),
           _resident(wu.shape), _resident(wd.shape), _resident((1, d))],
        out_specs=pl.BlockSpec((tm, d), row),
        out_shape=jax.ShapeDtypeStruct((n, d), F32),
        scratch_shapes=[pltpu.VMEM((tm, dff), BF16), pltpu.VMEM((2, tm, wa), BF16),
                        pltpu.VMEM((len(dilations), 2, wa // LANES, tm, LANES), F32),
                        pltpu.VMEM((len(dilations), 2, ST_LANES // LANES, tm, LANES), F32)],
        compiler_params=_cparams("arbitrary"),
        name="out_ffn",
    )(x, ycc, ys, *att_outs, *att_stats, gna.reshape(1, wa), wo, gf.reshape(1, d), wg, wu, wd,
      gl.reshape(1, d))


def _rope_tables(seq, wd):
    half = HEAD_DIM // 2
    inv = ROPE_THETA ** (-jnp.arange(0, HEAD_DIM, 2, dtype=F32) / HEAD_DIM)
    ang = jnp.arange(seq, dtype=F32)[:, None] * inv[None, :]
    cos, sin = jnp.cos(ang), jnp.sin(ang)
    reps = wd // HEAD_DIM
    cos_t = jnp.tile(jnp.concatenate([cos, cos], axis=1), (1, reps))
    sin_t = jnp.tile(jnp.concatenate([-sin, sin], axis=1), (1, reps))
    scale = HEAD_DIM ** -0.5
    return cos_t * scale, sin_t * scale, cos_t, sin_t


def kernel(x, norm_mix_g, w_in, conv3_w, cfm_dw_w, cfm_dw_b, cfm_ln_g, cfm_ln_b, s5_a_re, s5_a_im, s5_log_dt, s5_b_re, s5_b_im, s5_c_re, s5_c_im, s5_d, s5_glu_w, grp_norm_g, w_out, norm_ffn_g, w_gate, w_up, w_down, final_norm_g):
    batch, seq, d_model = x.shape
    depth = w_in.shape[0]
    wmix = w_in.shape[2] // 9
    n = batch * seq
    assert wmix == 2 * 8 * SSM_CH and all(w // d == ATT_BLK for w, d in DILATED_CFG)
    assert seq % (DILATED_CFG[-1][1] * ATT_BLK) == 0 and seq % 512 == 0
    tm = 512
    group = math.gcd(batch, 8)
    rope_tabs = _rope_tables(seq, wmix)

    dilations = tuple(dil for _, dil in DILATED_CFG)
    npat = len(dilations)

    w_in, s5_glu_w, w_out, w_gate, w_up, w_down = (
        a.astype(BF16) for a in (w_in, s5_glu_w, w_out, w_gate, w_up, w_down))
    ssm_mats = jax.vmap(_ssm_mats)(s5_a_re, s5_a_im, s5_log_dt, s5_b_re, s5_b_im, s5_c_re, s5_c_im)

    xf = x.reshape(n, d_model)
    for layer in range(depth):
        pc, pf, u, *qkv = _in_proj(xf, norm_mix_g[layer], w_in[layer], rope_tabs,
                                   batch, seq, tm, dilations, group)
        gn = grp_norm_g[layer]
        ycc = _conv_mixers(pc, pf, conv3_w[layer], cfm_dw_w[layer], cfm_dw_b[layer],
                           cfm_ln_g[layer], cfm_ln_b[layer], gn[:2 * wmix], batch, seq, 512)
        ys = _ssm_mixer(u, [a[layer] for a in ssm_mats], s5_d[layer], s5_glu_w[layer],
                        gn[2 * wmix:3 * wmix], seq, tm // SSM_CHUNK)
        outs, stats = zip(*[_attention(qkv[p], qkv[npat + p], qkv[2 * npat + p])
                            for p in range(npat)])
        xf = _out_ffn(xf, ycc, ys, outs, stats, gn[3 * wmix:], w_out[layer], norm_ffn_g[layer],
                      w_gate[layer], w_up[layer], w_down[layer], final_norm_g,
                      layer == depth - 1, tm)
    return xf.reshape(batch, seq, d_model)
```

```python
import functools
import math

import jax
import jax.numpy as jnp
from jax import lax
from jax.experimental import pallas as pl
from jax.experimental.pallas import tpu as pltpu

F32 = jnp.float32
BF16 = jnp.bfloat16

EPS = 1e-6
HEAD_DIM = 64
SHORT_K = 3
CFM_K = 31
SSM_CH = 16
SSM_STATE = 64
SSM_CHUNK = 8
DILATED_CFG = ((128, 1), (512, 4), (2048, 16))
ROPE_THETA = 10000.0
ATT_BLK = 128
NEG_BIG = -1e30
VMEM_LIMIT = 56 * 1024 * 1024


def _cparams(*sem):
    return pltpu.CompilerParams(dimension_semantics=sem, vmem_limit_bytes=VMEM_LIMIT)


def _rms(x, g):
    return x * lax.rsqrt(jnp.mean(x * x, axis=-1, keepdims=True) + EPS) * g


def _resident(shape):
    nd = len(shape)
    return pl.BlockSpec(shape, lambda *_: (0,) * nd, pipeline_mode=pl.Buffered(1))


LANES = 128


def _to_slabs(scr, val):
    for s in range(scr.shape[0]):
        scr[s] = val[:, s * LANES:(s + 1) * LANES]


def _rows_strided(scr, start, count, stride):
    return jnp.concatenate([scr[s, pl.ds(start, count, stride=stride), :]
                            for s in range(scr.shape[0])], axis=1)


def _in_proj_body(x_ref, g_ref, w_ref, cq_ref, sq_ref, ck_ref, sk_ref, pc_ref, pf_ref, u_ref,
                  *rest, wmix, dilations):
    qkv_refs, scr = rest[:-1], rest[-1]
    tm = x_ref.shape[0]
    h = _rms(x_ref[...], g_ref[...]).astype(BF16)

    def proj(lo, width):
        return jnp.dot(h, w_ref[:, lo:lo + width], preferred_element_type=F32)

    _to_slabs(scr.at[3, 0], proj(5 * wmix, wmix))
    for t in range(SSM_CHUNK):
        u_ref[:, t * wmix:(t + 1) * wmix] = _rows_strided(
            scr.at[3, 0], t, tm // SSM_CHUNK, SSM_CHUNK).astype(BF16)

    lane = lax.broadcasted_iota(jnp.int32, (1, wmix), 1)
    first_half = (lane % HEAD_DIM) < (HEAD_DIM // 2)

    def rope(t, c_ref, s_ref):
        partner = jnp.where(first_half, pltpu.roll(t, wmix - HEAD_DIM // 2, 1),
                            pltpu.roll(t, HEAD_DIM // 2, 1))
        return t * c_ref[...] + partner * s_ref[...]

    vals = (rope(proj(6 * wmix, wmix), cq_ref, sq_ref), rope(proj(7 * wmix, wmix), ck_ref, sk_ref),
            proj(8 * wmix, wmix))
    for i, val in enumerate(vals):
        outs = qkv_refs[i * len(dilations):(i + 1) * len(dilations)]
        prev, stage = 1, 0
        for n_d, (d, o_ref) in enumerate(zip(dilations, outs)):
            if d == 1:
                o_ref[0] = val.astype(BF16)
                continue
            src, dst = scr.at[i, stage % 2], scr.at[i, 1 - stage % 2]
            if stage == 0:
                _to_slabs(src, val)
            ratio, rows = d // prev, tm // d
            for r in range(prev):
                for a in range(ratio):
                    part = _rows_strided(src, r * (tm // prev) + a, rows, ratio)
                    o_ref[r + prev * a] = part.astype(BF16)
                    if n_d + 1 < len(dilations):
                        _to_slabs(dst.at[:, pl.ds((r + prev * a) * rows, rows)], part)
            prev, stage = d, stage + 1

    pc_ref[...] = proj(0, 3 * wmix).astype(BF16)
    pf_ref[...] = proj(3 * wmix, 2 * wmix).astype(BF16)


def _in_proj(x, g, w, rope_tabs, batch, seq, tm, dilations, group):
    n, d = x.shape
    nin = w.shape[1]
    wmix = nin // 9
    per_seq = seq // tm
    row = lambda i: (i, 0)
    tab = lambda i: (i % per_seq, 0)
    flat = [(n, 3 * wmix), (n, 2 * wmix)]
    flat_blocks = [(tm, 3 * wmix), (tm, 2 * wmix)]
    uw = SSM_CHUNK * wmix
    u_shape = (batch // group, seq // SSM_CHUNK, group * uw)
    u_spec = pl.BlockSpec((None, tm // SSM_CHUNK, uw),
                          lambda i: (i // per_seq // group, i % per_seq, i // per_seq % group))
    res_shapes = [(batch, dl, seq // dl, wmix) for dl in dilations] * 3
    res_specs = [pl.BlockSpec((None, dl, tm // dl, wmix),
                              lambda i: (i // per_seq, 0, i % per_seq, 0)) for dl in dilations] * 3
    outs = pl.pallas_call(
        functools.partial(_in_proj_body, wmix=wmix, dilations=dilations),
        grid=(n // tm,),
        in_specs=[pl.BlockSpec((tm, d), row), _resident((1, d)), _resident((d, nin))]
        + [pl.BlockSpec((tm, wmix), tab)] * 4,
        out_specs=[pl.BlockSpec(b, row) for b in flat_blocks] + [u_spec] + res_specs,
        out_shape=[jax.ShapeDtypeStruct(s, BF16) for s in flat + [u_shape] + res_shapes],
        scratch_shapes=[pltpu.VMEM((4, 2, wmix // LANES, tm, LANES), F32)],
        compiler_params=_cparams("parallel"),
        name="in_proj",
    )(x, g.reshape(1, d), w, *rope_tabs)
    return outs


CONV_ROWS = 32
CFM_HALO = 32
SHORT_HALO = 16


def _conv_body(pc_ref, pch_ref, pf_ref, pfh_ref, w3_ref, wdw_ref, bdw_ref, lng_ref, lnb_ref,
               gn_ref, o_ref, zb3, zs, *, wmix):
    t_rows = pc_ref.shape[0]
    keep = (pl.program_id(1) > 0).astype(F32)

    pc = pc_ref[...].astype(F32)
    ch = pc[:, 2 * wmix:] * pc[:, :wmix]
    hal = pch_ref[...].astype(F32)
    zb3[0:SHORT_HALO, :] = hal[:, 2 * wmix:] * hal[:, :wmix] * keep
    zb3[SHORT_HALO:, :] = ch
    conv = w3_ref[SHORT_K - 1:SHORT_K, :] * ch
    for k in range(SHORT_K - 1):
        off = SHORT_HALO - (SHORT_K - 1) + k
        conv = conv + w3_ref[k:k + 1, :] * zb3[off:off + t_rows, :]
    y_conv = pc[:, wmix:2 * wmix] * conv
    o_ref[:, :wmix] = _rms(y_conv, gn_ref[:, :wmix]).astype(BF16)

    pf = pf_ref[...].astype(F32)
    ph = pfh_ref[...].astype(F32)
    zs[0, 0:CFM_HALO, :] = ph[:, :wmix] * jax.nn.sigmoid(ph[:, wmix:]) * keep
    zs[0, CFM_HALO:, :] = pf[:, :wmix] * jax.nn.sigmoid(pf[:, wmix:])
    span = t_rows + CFM_HALO - 8
    for b in range(1, 8):
        zs[b, 0:span, :] = zs[0, b:b + span, :]

    base = CFM_HALO - (CFM_K - 1)

    def rows(c, carry):
        r0 = pl.multiple_of(c * CONV_ROWS, CONV_ROWS)
        accs = [None] * 4
        for b in range(8):
            ks = [k for k in range(CFM_K) if (base + k) % 8 == b]
            hi = max((base + k) // 8 for k in ks)
            win = zs[b, pl.ds(r0, CONV_ROWS + 8 * hi), :].reshape(CONV_ROWS // 8 + hi, 8, wmix)
            for k in ks:
                a = (base + k) // 8
                term = wdw_ref[k] * win[a:a + CONV_ROWS // 8]
                accs[k % 4] = term if accs[k % 4] is None else accs[k % 4] + term
        acc = (accs[0] + accs[1]) + (accs[2] + accs[3])
        zb3[pl.ds(r0, CONV_ROWS), :] = acc.reshape(CONV_ROWS, wmix) + bdw_ref[...]
        return carry

    lax.fori_loop(0, t_rows // CONV_ROWS, rows, 0, unroll=2)

    acc = zb3[0:t_rows, :]
    mu = jnp.mean(acc, axis=-1, keepdims=True)
    xc = acc - mu
    var = jnp.mean(xc * xc, axis=-1, keepdims=True)
    y = xc * lax.rsqrt(var + EPS) * lng_ref[...] + lnb_ref[...]
    y = y * jax.nn.sigmoid(y)
    o_ref[:, wmix:] = _rms(y, gn_ref[:, wmix:]).astype(BF16)


def _conv_mixers(pc, pf, w3, wdw, bdw, lng, lnb, gn, batch, seq, tt):
    n = pc.shape[0]
    wmix = pf.shape[1] // 2
    per_seq = seq // tt
    row = lambda b, i: (b * per_seq + i, 0)

    def halo(rows_):
        return lambda b, i: (jnp.maximum((b * per_seq + i) * (tt // rows_) - 1, 0), 0)

    vec = lambda a: a.reshape(1, -1)
    return pl.pallas_call(
        functools.partial(_conv_body, wmix=wmix),
        grid=(batch, per_seq),
        in_specs=[pl.BlockSpec((tt, 3 * wmix), row),
                  pl.BlockSpec((SHORT_HALO, 3 * wmix), halo(SHORT_HALO)),
                  pl.BlockSpec((tt, 2 * wmix), row),
                  pl.BlockSpec((CFM_HALO, 2 * wmix), halo(CFM_HALO)),
                  _resident((SHORT_K, wmix)), _resident((CFM_K, 8, wmix)),
                  _resident((1, wmix)), _resident((1, wmix)), _resident((1, wmix)),
                  _resident((1, 2 * wmix))],
        out_specs=pl.BlockSpec((tt, 2 * wmix), row),
        out_shape=jax.ShapeDtypeStruct((n, 2 * wmix), BF16),
        scratch_shapes=[pltpu.VMEM((tt + SHORT_HALO, wmix), F32),
                        pltpu.VMEM((8, tt + CFM_HALO, wmix), F32)],
        compiler_params=_cparams("parallel", "arbitrary"),
        name="conv_mixers",
    )(pc, pc, pf, pf, w3, jnp.broadcast_to(wdw[:, None, :], (CFM_K, 8, wmix)), vec(bdw), vec(lng),
      vec(lnb), vec(gn))


def _cmul(ar, ai, br, bi):
    return ar * br - ai * bi, ar * bi + ai * br


def _ssm_discretise(a_re, a_im, log_dt, b_re, b_im, c_re, c_im):
    ng, ns = a_re.shape
    nh = b_re.shape[-1]
    tc = SSM_CHUNK
    dt = jnp.exp(log_dt)[:, None]
    mag = jnp.exp(a_re * dt)
    lr, li = mag * jnp.cos(a_im * dt), mag * jnp.sin(a_im * dt)
    den = a_re * a_re + a_im * a_im
    nr, ni = lr - 1.0, li
    fr = (nr * a_re + ni * a_im) / den
    fi = (ni * a_re - nr * a_im) / den
    bbr = fr[..., None] * b_re - fi[..., None] * b_im
    bbi = fr[..., None] * b_im + fi[..., None] * b_re

    pr, pi = [jnp.ones_like(lr)], [jnp.zeros_like(li)]
    for _ in range(tc):
        r, i = _cmul(pr[-1], pi[-1], lr, li)
        pr.append(r)
        pi.append(i)
    pwr, pwi = jnp.stack(pr), jnp.stack(pi)

    lbr, lbi = _cmul(pwr[:tc, :, :, None], pwi[:tc, :, :, None], bbr[None], bbi[None])
    lbt = jnp.stack([lbr, lbi]).transpose(0, 1, 2, 4, 3).reshape(2, tc, ng * nh, ns)
    ct = jnp.stack([c_re, c_im]).transpose(0, 1, 3, 2).reshape(2, ng * ns, nh)
    clr, cli = _cmul(c_re[None], c_im[None], pwr[1:, :, None, :], pwi[1:, :, None, :])
    clt = jnp.stack([clr, cli]).transpose(0, 1, 2, 4, 3).reshape(2, tc, ng * ns, nh)
    lam8 = jnp.concatenate([pwr[tc].reshape(2, 1, -1), pwi[tc].reshape(2, 1, -1)], axis=2)
    return lbt, ct, clt, lam8


def _split_bf16(x, terms):
    parts = []
    for _ in range(terms):
        part = x.astype(BF16)
        parts.append(part)
        x = x - part.astype(F32)
    return parts


def _ssm_fold_body(lbt_ref, ct_ref, clt_ref, m_ref, w_ref, g_ref):
    tc = lbt_ref.shape[1]
    rows, ns = lbt_ref.shape[2], lbt_ref.shape[3]
    nh = ct_ref.shape[2]
    gh = rows // nh
    srows = gh * ns

    def grid2(shape):
        return (lax.broadcasted_iota(jnp.int32, shape, 0), lax.broadcasted_iota(jnp.int32, shape, 1))

    r, c = grid2((ns, srows))
    spread_p = jnp.where(c % ns == r, 1.0, 0.0).astype(BF16)
    r, c = grid2((nh, rows))
    spread_h = jnp.where(c % nh == r, 1.0, 0.0).astype(BF16)
    r, c = grid2((rows, srows))
    mask_w = r // nh == c // ns
    r, c = grid2((rows, rows))
    mask_m = r // nh == c // nh
    r, c = grid2((srows, rows))
    mask_g = r // ns == c // nh

    def dot(a, b):
        return jnp.dot(a, b, preferred_element_type=F32)

    lb = [[jnp.where(mask_w, sum(dot(t, spread_p) for t in _split_bf16(lbt_ref[ri, d], 3)), 0.0)
           for d in range(tc)] for ri in range(2)]
    for tau in range(tc):
        w_ref[tau * rows:(tau + 1) * rows, :] = jnp.concatenate(
            [lb[0][tc - 1 - tau], lb[1][tc - 1 - tau]], axis=1).astype(BF16)

    c_parts = [_split_bf16(ct_ref[ri], 2) for ri in range(2)]

    def dot_f32(x, y_parts):
        xh, xl = _split_bf16(x, 2)
        return dot(xh, y_parts[0]) + dot(xh, y_parts[1]) + dot(xl, y_parts[0])

    zero = jnp.zeros((rows, rows), BF16)
    blocks = []
    for d in range(tc):
        kd = dot_f32(lb[0][d], c_parts[0]) - dot_f32(lb[1][d], c_parts[1])
        blocks.append(jnp.where(mask_m, dot(kd.astype(BF16), spread_h), 0.0).astype(BF16))
    for tau in range(tc):
        for t in range(tc):
            m_ref[tau * rows:(tau + 1) * rows, t * rows:(t + 1) * rows] = (
                blocks[t - tau] if t >= tau else zero)

    for t in range(tc):
        for ri in range(2):
            blk = jnp.where(mask_g, dot(clt_ref[ri, t].astype(BF16), spread_h), 0.0)
            g_ref[ri * srows:(ri + 1) * srows, t * rows:(t + 1) * rows] = (
                (-blk if ri else blk).astype(BF16))


def _ssm_fold(lbt, ct, clt):
    depth, _, tc, ghn, ns = lbt.shape
    nh = ct.shape[3]
    rows = ghn // 2
    srows = ct.shape[2] // 2
    out = jax.ShapeDtypeStruct((depth, 2, tc * rows, tc * rows), BF16)
    out_spec = pl.BlockSpec((None, None, tc * rows, tc * rows), lambda l, j: (l, j, 0, 0))
    assert 2 * srows == tc * rows
    return pl.pallas_call(
        _ssm_fold_body,
        grid=(depth, 2),
        in_specs=[pl.BlockSpec((None, 2, tc, rows, ns), lambda l, j: (l, 0, 0, j, 0)),
                  pl.BlockSpec((None, 2, srows, nh), lambda l, j: (l, 0, j, 0)),
                  pl.BlockSpec((None, 2, tc, srows, nh), lambda l, j: (l, 0, 0, j, 0))],
        out_specs=[out_spec] * 3,
        out_shape=[out] * 3,
        compiler_params=_cparams("parallel", "parallel"),
        name="ssm_fold",
    )(lbt, ct, clt)


def _ssm_body(u_ref, m_ref, w_ref, g_ref, a_ref, d_ref, wglu_ref, gn_ref, o_ref, state, zbuf, sbuf,
              zrow, ysc, uscr, tok, *, half, group):
    steps = u_ref.shape[0]
    width = u_ref.shape[1] // group
    ch = 2 * half
    nst = a_ref.shape[2] // 2
    assert half == LANES

    for b in range(group):
        part = u_ref[:, b * width:(b + 1) * width].astype(F32)
        for s in range(width // LANES):
            uscr[s, pl.ds(b, steps, stride=group), :] = part[:, s * LANES:(s + 1) * LANES]

    @pl.when(pl.program_id(1) == 0)
    def _():
        state[...] = jnp.zeros(state.shape, F32)

    lanes = [[slice(t * ch + j * half, t * ch + (j + 1) * half) for t in range(SSM_CHUNK)]
             for j in range(2)]
    lhs = [jnp.concatenate([uscr[s.start // LANES] for s in lanes[j]], axis=1).astype(BF16)
           for j in range(2)]
    for j in range(2):
        zbuf[j] = jnp.dot(lhs[j], w_ref[j], preferred_element_type=F32)

    lam = [(jnp.broadcast_to(a_ref[j, :, :nst], (group, nst)),
            jnp.broadcast_to(a_ref[j, :, nst:], (group, nst))) for j in range(2)]

    def chunk(r, carry):
        r0 = pl.multiple_of(r * group, group)
        out = []
        for j in range(2):
            sr, si = carry[2 * j], carry[2 * j + 1]
            sbuf[j, pl.ds(r0, group), :] = jnp.concatenate([sr, si], axis=1)
            z = zbuf[j, pl.ds(r0, group), :]
            ar, ai = lam[j]
            out += [ar * sr - ai * si + z[:, :nst], ar * si + ai * sr + z[:, nst:]]
        return tuple(out)

    final = lax.fori_loop(0, steps, chunk, tuple(state[i] for i in range(4)), unroll=8)
    for i in range(4):
        state[i] = final[i]

    for j in range(2):
        y = (jnp.dot(lhs[j], m_ref[j], preferred_element_type=F32)
             + jnp.dot(sbuf[j].astype(BF16), g_ref[j], preferred_element_type=F32))
        for t, s in enumerate(lanes[j]):
            yy = y[:, t * half:(t + 1) * half] + d_ref[:, s] * uscr[s.start // LANES]
            zrow[:, s] = jax.nn.gelu(yy).astype(BF16)

    for t in range(SSM_CHUNK):
        gl = jnp.dot(zrow[:, t * ch:(t + 1) * ch], wglu_ref[...], preferred_element_type=F32)
        _to_slabs(ysc, _rms(gl[:, :ch] * jax.nn.sigmoid(gl[:, ch:]), gn_ref[...]))
        for b in range(group):
            for s in range(ysc.shape[0]):
                tok[b, s, pl.ds(t, steps, stride=SSM_CHUNK), :] = (
                    ysc[s, pl.ds(b, steps, stride=group), :])
    for b in range(group):
        o_ref[b] = jnp.concatenate([tok[b, s] for s in range(tok.shape[1])], axis=1).astype(BF16)


def _ssm_mixer(u_rows, mats, d_skip, wglu, gn, seq, tile_rows):
    n_groups = u_rows.shape[0]
    width = SSM_CHUNK * wglu.shape[0]
    ch = width // SSM_CHUNK
    m, w, g, a = mats
    group = u_rows.shape[2] // width
    rows_ = tile_rows * group
    d_row = jnp.tile(d_skip, SSM_CHUNK).reshape(1, width)
    nst = a.shape[2] // 2
    out = pl.pallas_call(
        functools.partial(_ssm_body, half=ch // 2, group=group),
        grid=(n_groups, seq // SSM_CHUNK // tile_rows),
        in_specs=[pl.BlockSpec((None, tile_rows, group * width), lambda b, t: (b, t, 0)),
                  _resident(m.shape), _resident(w.shape), _resident(g.shape),
                  _resident(a.shape), _resident((1, width)), _resident(wglu.shape),
                  _resident((1, ch))],
        out_specs=pl.BlockSpec((group, tile_rows * SSM_CHUNK, ch), lambda b, t: (b, t, 0)),
        out_shape=jax.ShapeDtypeStruct((n_groups * group, seq, ch), BF16),
        scratch_shapes=[pltpu.VMEM((4, group, nst), F32),
                        pltpu.VMEM((2, rows_, 2 * nst), F32), pltpu.VMEM((2, rows_, 2 * nst), F32),
                        pltpu.VMEM((rows_, width), BF16),
                        pltpu.VMEM((ch // LANES, rows_, LANES), F32),
                        pltpu.VMEM((width // LANES, rows_, LANES), F32),
                        pltpu.VMEM((group, ch // LANES, tile_rows * SSM_CHUNK, LANES), F32)],
        compiler_params=_cparams("parallel", "arbitrary"),
        name="ssm_mixer",
    )(u_rows, m, w, g, a, d_row, wglu, gn.reshape(1, ch))
    return out.reshape(n_groups * group * seq, ch)


def _attn_body(q_ref, k_ref, v_ref, o_ref, st_ref, sbuf, pbuf, *, n_heads, qb):
    blk = ATT_BLK
    wd = q_ref.shape[1]
    nblk = q_ref.shape[0] // blk

    qi = lax.broadcasted_iota(jnp.int32, (blk, 2 * blk), 0)
    kj = lax.broadcasted_iota(jnp.int32, (blk, 2 * blk), 1)
    rel = blk + qi - kj
    bias_any = jnp.where((rel >= 0) & (rel <= blk), 0.0, NEG_BIG)
    bias_first = jnp.where(kj <= qi, 0.0, NEG_BIG)
    head_of_lane = lax.broadcasted_iota(jnp.int32, (1, wd), 1) // HEAD_DIM
    st_lane = lax.broadcasted_iota(jnp.int32, (1, st_ref.shape[1]), 1)

    def rows_of(f):
        return f * blk if isinstance(f, int) else pl.multiple_of(f * blk, blk)

    def window_of(f):
        first = (f % qb) == 0
        if isinstance(f, int):
            return (f if first else f - 1) * blk, first
        return pl.multiple_of((f - jnp.where(first, 0, 1)) * blk, blk), first

    def scores(f, slot):
        q = q_ref[pl.ds(rows_of(f), blk), :]
        qs = jnp.concatenate([jnp.where(head_of_lane == h, q, jnp.zeros_like(q))
                              for h in range(n_heads)], axis=0)
        sbuf[slot] = lax.dot_general(qs, k_ref[pl.ds(window_of(f)[0], 2 * blk), :],
                                     (((1,), (1,)), ((), ())), preferred_element_type=F32)

    def softmax(f, slot):
        first = window_of(f)[1]
        bias = (bias_first if first else bias_any) if isinstance(f, int) else jnp.where(
            first, bias_first, bias_any)
        stats = jnp.zeros((blk, st_ref.shape[1]), F32)
        for h in range(n_heads):
            s = sbuf[slot, h * blk:(h + 1) * blk, :] + bias
            m = jnp.max(s, axis=-1, keepdims=True)
            p = jnp.exp(s - m)
            pbuf[slot, h * blk:(h + 1) * blk, :] = p.astype(BF16)
            stats = jnp.where(st_lane == h, m, stats)
            stats = jnp.where(st_lane == n_heads + h, jnp.sum(p, axis=-1, keepdims=True), stats)
        st_ref[pl.ds(rows_of(f), blk), :] = stats

    def values(f, slot):
        o_all = jnp.dot(pbuf[slot], v_ref[pl.ds(window_of(f)[0], 2 * blk), :],
                        preferred_element_type=F32)
        o = o_all[0:blk]
        for h in range(1, n_heads):
            o = jnp.where(head_of_lane == h, o_all[h * blk:(h + 1) * blk], o)
        o_ref[pl.ds(rows_of(f), blk), :] = o.astype(BF16)

    def step(f, slot):
        scores(f, slot)
        softmax(f - 1, 1 - slot)
        values(f - 2, slot)

    scores(0, 0)
    scores(1, 1)
    softmax(0, 0)

    def pair(it, carry):
        step(2 * it + 2, 0)
        step(2 * it + 3, 1)
        return carry

    lax.fori_loop(0, (nblk - 2) // 2, pair, 0)
    softmax(nblk - 1, 1)
    values(nblk - 2, 0)
    values(nblk - 1, 1)


ST_LANES = LANES


def _attention(q, k, v):
    batch, dilation, rows_, wd = q.shape
    seq = dilation * rows_
    n_heads = wd // HEAD_DIM
    qb = rows_ // ATT_BLK
    assert qb >= 2 and qb & (qb - 1) == 0
    flat = lambda a: a.reshape(batch, seq, wd)
    spec = pl.BlockSpec((None, seq, wd), lambda b: (b, 0, 0))
    o, st = pl.pallas_call(
        functools.partial(_attn_body, n_heads=n_heads, qb=qb),
        grid=(batch,),
        in_specs=[spec, spec, spec],
        out_specs=[spec, pl.BlockSpec((None, seq, ST_LANES), lambda b: (b, 0, 0))],
        out_shape=[jax.ShapeDtypeStruct((batch, seq, wd), BF16),
                   jax.ShapeDtypeStruct((batch, seq, ST_LANES), F32)],
        scratch_shapes=[pltpu.VMEM((2, n_heads * ATT_BLK, 2 * ATT_BLK), F32),
                        pltpu.VMEM((2, n_heads * ATT_BLK, 2 * ATT_BLK), BF16)],
        compiler_params=_cparams("parallel"),
        name=f"attn_d{dilation}",
    )(flat(q), flat(k), flat(v))
    return (o.reshape(batch, dilation, rows_, wd), st.reshape(batch, dilation, rows_, ST_LANES))


def _token_order(block, d, bufs):
    slabs, tm, _ = bufs[0].shape
    stage = 0
    while d > 1:
        f = 4 if d % 4 == 0 else d
        nxt, rows, dst = d // f, tm // d, bufs[stage % 2]
        for r1 in range(nxt):
            for a in range(f):
                blk = block(r1 + nxt * a)
                for s in range(slabs):
                    dst[s, pl.ds(r1 * (tm // nxt) + a, rows, stride=f), :] = (
                        blk[:, s * LANES:(s + 1) * LANES])

        def block(r, dst=dst, size=tm // nxt):
            return jnp.concatenate([dst[s, pl.ds(r * size, size), :] for s in range(slabs)], axis=1)

        d, stage = nxt, stage + 1
    return block(0)


def _att_mix(o_refs, s_refs, gn_ref, oscr, sscr, dilations):
    tm, wd = oscr.shape[3], oscr.shape[2] * LANES
    n_heads = wd // HEAD_DIM
    outs, lses = [], []
    for p, d in enumerate(dilations):
        outs.append(_token_order(lambda r, p=p: o_refs[p][r].astype(F32), d,
                                 (oscr.at[p, 0], oscr.at[p, 1])))
        lses.append(_token_order(lambda r, p=p: s_refs[p][r], d, (sscr.at[p, 0], sscr.at[p, 1])))
    m = functools.reduce(jnp.maximum, lses)
    head_of_lane = lax.broadcasted_iota(jnp.int32, (1, wd), 1) // HEAD_DIM

    def widen(x, lane0):
        wide = jnp.zeros((tm, wd), F32)
        for h in range(n_heads):
            wide = jnp.where(head_of_lane == h, x[:, lane0 + h:lane0 + h + 1], wide)
        return wide

    num = jnp.zeros((tm, wd), F32)
    den = jnp.zeros((tm, wd), F32)
    for st, o in zip(lses, outs):
        e = widen(jnp.exp(st - m), 0)
        num = num + e * o
        den = den + e * widen(st, n_heads)
    return _rms(num / den, gn_ref[...]).astype(BF16)


FF_CHUNK = 256


def _out_ffn_body(x_ref, ycc_ref, ys_ref, *rest, final_norm, dilations):
    npat = len(dilations)
    o_refs, s_refs = rest[:npat], rest[npat:2 * npat]
    (gna_ref, wo_ref, gf_ref, wg_ref, wu_ref, wd_ref, gl_ref, o_ref, act, ya, oscr,
     sscr) = rest[2 * npat:]
    g = pl.program_id(0)

    @pl.when(g == 0)
    def _():
        ya[1] = jnp.zeros(ya.shape[1:], BF16)

    ya_prev = ya[(g + 1) % 2]
    ya[g % 2] = _att_mix(o_refs, s_refs, gna_ref, oscr, sscr, dilations)

    mixed = jnp.concatenate([ycc_ref[...], ys_ref[...], ya_prev], axis=1)
    x1 = x_ref[...] + jnp.dot(mixed, wo_ref[...], preferred_element_type=F32)
    o_ref[...] = x1
    h = _rms(x1, gf_ref[...]).astype(BF16)
    dff = wg_ref.shape[1]
    for c in range(dff // FF_CHUNK):
        sl = slice(c * FF_CHUNK, (c + 1) * FF_CHUNK)
        g = jnp.dot(h, wg_ref[:, sl], preferred_element_type=F32)
        u = jnp.dot(h, wu_ref[:, sl], preferred_element_type=F32)
        act[:, sl] = (g * jax.nn.sigmoid(g) * u).astype(BF16)
    x2 = o_ref[...] + jnp.dot(act[...], wd_ref[...], preferred_element_type=F32)
    if final_norm:
        x2 = _rms(x2, gl_ref[...])
    o_ref[...] = x2


def _out_ffn(x, ycc, ys, att_outs, att_stats, gna, wo, gf, wg, wu, wd, gl, final_norm, tm):
    n, d = x.shape
    dff = wg.shape[1]
    batch, _, _, wa = att_outs[0].shape
    seq = n // batch
    dilations = tuple(o.shape[1] for o in att_outs)
    per_seq = seq // tm
    tiles = n // tm
    row = lambda g: (jnp.maximum(g - 1, 0), 0)

    def res_spec(dl, width):
        def index(g):
            t = jnp.minimum(g, tiles - 1)
            return (t // per_seq, 0, t % per_seq, 0)
        return pl.BlockSpec((None, dl, tm // dl, width), index)

    return pl.pallas_call(
        functools.partial(_out_ffn_body, final_norm=final_norm, dilations=dilations),
        grid=(tiles + 1,),
        in_specs=[pl.BlockSpec((tm, d), row), pl.BlockSpec((tm, ycc.shape[1]), row),
                  pl.BlockSpec((tm, ys.shape[1]), row)]
        + [res_spec(dl, wa) for dl in dilations] + [res_spec(dl, ST_LANES) for dl in dilations]
        + [_resident((1, wa)), _resident(wo.shape), _resident((1, d)), _resident(wg.shape),
           _resident(wu.shape), _resident(wd.shape), _resident((1, d))],
        out_specs=pl.BlockSpec((tm, d), row),
        out_shape=jax.ShapeDtypeStruct((n, d), F32),
        scratch_shapes=[pltpu.VMEM((tm, dff), BF16), pltpu.VMEM((2, tm, wa), BF16),
                        pltpu.VMEM((len(dilations), 2, wa // LANES, tm, LANES), F32),
                        pltpu.VMEM((len(dilations), 2, ST_LANES // LANES, tm, LANES), F32)],
        compiler_params=_cparams("arbitrary"),
        name="out_ffn",
    )(x, ycc, ys, *att_outs, *att_stats, gna.reshape(1, wa), wo, gf.reshape(1, d), wg, wu, wd,
      gl.reshape(1, d))


def _rope_tables(seq, wd):
    half = HEAD_DIM // 2
    inv = ROPE_THETA ** (-jnp.arange(0, HEAD_DIM, 2, dtype=F32) / HEAD_DIM)
    ang = jnp.arange(seq, dtype=F32)[:, None] * inv[None, :]
    cos, sin = jnp.cos(ang), jnp.sin(ang)
    reps = wd // HEAD_DIM
    cos_t = jnp.tile(jnp.concatenate([cos, cos], axis=1), (1, reps))
    sin_t = jnp.tile(jnp.concatenate([-sin, sin], axis=1), (1, reps))
    scale = HEAD_DIM ** -0.5
    return cos_t * scale, sin_t * scale, cos_t, sin_t


def kernel(x, norm_mix_g, w_in, conv3_w, cfm_dw_w, cfm_dw_b, cfm_ln_g, cfm_ln_b, s5_a_re, s5_a_im, s5_log_dt, s5_b_re, s5_b_im, s5_c_re, s5_c_im, s5_d, s5_glu_w, grp_norm_g, w_out, norm_ffn_g, w_gate, w_up, w_down, final_norm_g):
    batch, seq, d_model = x.shape
    depth = w_in.shape[0]
    wmix = w_in.shape[2] // 9
    n = batch * seq
    assert wmix == 2 * 8 * SSM_CH and all(w // d == ATT_BLK for w, d in DILATED_CFG)
    assert seq % (DILATED_CFG[-1][1] * ATT_BLK) == 0 and seq % 512 == 0
    tm = 512
    group = math.gcd(batch, 8)
    rope_tabs = _rope_tables(seq, wmix)

    dilations = tuple(dil for _, dil in DILATED_CFG)
    npat = len(dilations)

    w_in, s5_glu_w, w_out, w_gate, w_up, w_down = (
        a.astype(BF16) for a in (w_in, s5_glu_w, w_out, w_gate, w_up, w_down))
    lbt, ct, clt, lam8 = jax.vmap(_ssm_discretise)(s5_a_re, s5_a_im, s5_log_dt, s5_b_re, s5_b_im,
                                                   s5_c_re, s5_c_im)
    ssm_mats = (*_ssm_fold(lbt, ct, clt), lam8)

    xf = x.reshape(n, d_model)
    for layer in range(depth):
        pc, pf, u, *qkv = _in_proj(xf, norm_mix_g[layer], w_in[layer], rope_tabs,
                                   batch, seq, 2 * tm, dilations, group)
        gn = grp_norm_g[layer]
        ycc = _conv_mixers(pc, pf, conv3_w[layer], cfm_dw_w[layer], cfm_dw_b[layer],
                           cfm_ln_g[layer], cfm_ln_b[layer], gn[:2 * wmix], batch, seq, 512)
        ys = _ssm_mixer(u, [a[layer] for a in ssm_mats], s5_d[layer], s5_glu_w[layer],
                        gn[2 * wmix:3 * wmix], seq, tm // SSM_CHUNK)
        outs, stats = zip(*[_attention(qkv[p], qkv[npat + p], qkv[2 * npat + p])
                            for p in range(npat)])
        xf = _out_ffn(xf, ycc, ys, outs, stats, gn[3 * wmix:], w_out[layer], norm_ffn_g[layer],
                      w_gate[layer], w_up[layer], w_down[layer], final_norm_g,
                      layer == depth - 1, tm)
    return xf.reshape(batch, seq, d_model)
```

```python
import functools
import math

import jax
import jax.numpy as jnp
from jax import lax
from jax.experimental import pallas as pl
from jax.experimental.pallas import tpu as pltpu

F32 = jnp.float32
BF16 = jnp.bfloat16

EPS = 1e-6
HEAD_DIM = 64
SHORT_K = 3
CFM_K = 31
SSM_CH = 16
SSM_STATE = 64
SSM_CHUNK = 8
DILATED_CFG = ((128, 1), (512, 4), (2048, 16))
ROPE_THETA = 10000.0
ATT_BLK = 128
NEG_BIG = -1e30
VMEM_LIMIT = 56 * 1024 * 1024


def _cparams(*sem):
    return pltpu.CompilerParams(dimension_semantics=sem, vmem_limit_bytes=VMEM_LIMIT)


def _rms(x, g):
    return x * lax.rsqrt(jnp.mean(x * x, axis=-1, keepdims=True) + EPS) * g


def _resident(shape):
    nd = len(shape)
    return pl.BlockSpec(shape, lambda *_: (0,) * nd, pipeline_mode=pl.Buffered(1))


LANES = 128


def _to_slabs(scr, val):
    for s in range(scr.shape[0]):
        scr[s] = val[:, s * LANES:(s + 1) * LANES]


def _rows_strided(scr, start, count, stride):
    return jnp.concatenate([scr[s, pl.ds(start, count, stride=stride), :]
                            for s in range(scr.shape[0])], axis=1)


def _in_proj_body(x_ref, g_ref, w_ref, cq_ref, sq_ref, ck_ref, sk_ref, pc_ref, pf_ref, u_ref,
                  *rest, wmix, dilations):
    qkv_refs, scr = rest[:-1], rest[-1]
    tm = x_ref.shape[0]
    h = _rms(x_ref[...], g_ref[...]).astype(BF16)

    def proj(lo, width):
        return jnp.dot(h, w_ref[:, lo:lo + width], preferred_element_type=F32)

    _to_slabs(scr.at[3, 0], proj(5 * wmix, wmix))
    for t in range(SSM_CHUNK):
        u_ref[:, t * wmix:(t + 1) * wmix] = _rows_strided(
            scr.at[3, 0], t, tm // SSM_CHUNK, SSM_CHUNK).astype(BF16)

    lane = lax.broadcasted_iota(jnp.int32, (1, wmix), 1)
    first_half = (lane % HEAD_DIM) < (HEAD_DIM // 2)

    def rope(t, c_ref, s_ref):
        partner = jnp.where(first_half, pltpu.roll(t, wmix - HEAD_DIM // 2, 1),
                            pltpu.roll(t, HEAD_DIM // 2, 1))
        return t * c_ref[...] + partner * s_ref[...]

    vals = (rope(proj(6 * wmix, wmix), cq_ref, sq_ref), rope(proj(7 * wmix, wmix), ck_ref, sk_ref),
            proj(8 * wmix, wmix))
    for i, val in enumerate(vals):
        outs = qkv_refs[i * len(dilations):(i + 1) * len(dilations)]
        prev, stage = 1, 0
        for n_d, (d, o_ref) in enumerate(zip(dilations, outs)):
            if d == 1:
                o_ref[0] = val.astype(BF16)
                continue
            src, dst = scr.at[i, stage % 2], scr.at[i, 1 - stage % 2]
            if stage == 0:
                _to_slabs(src, val)
            ratio, rows = d // prev, tm // d
            for r in range(prev):
                for a in range(ratio):
                    part = _rows_strided(src, r * (tm // prev) + a, rows, ratio)
                    o_ref[r + prev * a] = part.astype(BF16)
                    if n_d + 1 < len(dilations):
                        _to_slabs(dst.at[:, pl.ds((r + prev * a) * rows, rows)], part)
            prev, stage = d, stage + 1

    pc_ref[...] = proj(0, 3 * wmix).astype(BF16)
    pf_ref[...] = proj(3 * wmix, 2 * wmix).astype(BF16)


def _in_proj(x, g, w, rope_tabs, batch, seq, tm, dilations, group):
    n, d = x.shape
    nin = w.shape[1]
    wmix = nin // 9
    per_seq = seq // tm
    row = lambda i: (i, 0)
    tab = lambda i: (i % per_seq, 0)
    flat = [(n, 3 * wmix), (n, 2 * wmix)]
    flat_blocks = [(tm, 3 * wmix), (tm, 2 * wmix)]
    uw = SSM_CHUNK * wmix
    u_shape = (batch // group, seq // SSM_CHUNK, group * uw)
    u_spec = pl.BlockSpec((None, tm // SSM_CHUNK, uw),
                          lambda i: (i // per_seq // group, i % per_seq, i // per_seq % group))
    res_shapes = [(batch, dl, seq // dl, wmix) for dl in dilations] * 3
    res_specs = [pl.BlockSpec((None, dl, tm // dl, wmix),
                              lambda i: (i // per_seq, 0, i % per_seq, 0)) for dl in dilations] * 3
    outs = pl.pallas_call(
        functools.partial(_in_proj_body, wmix=wmix, dilations=dilations),
        grid=(n // tm,),
        in_specs=[pl.BlockSpec((tm, d), row), _resident((1, d)), _resident((d, nin))]
        + [pl.BlockSpec((tm, wmix), tab)] * 4,
        out_specs=[pl.BlockSpec(b, row) for b in flat_blocks] + [u_spec] + res_specs,
        out_shape=[jax.ShapeDtypeStruct(s, BF16) for s in flat + [u_shape] + res_shapes],
        scratch_shapes=[pltpu.VMEM((4, 2, wmix // LANES, tm, LANES), F32)],
        compiler_params=_cparams("parallel"),
        name="in_proj",
    )(x, g.reshape(1, d), w, *rope_tabs)
    return outs


CONV_ROWS = 32
CFM_HALO = 32
SHORT_HALO = 16


def _conv_tile(pc_ref, pch_ref, pf_ref, pfh_ref, w3_ref, wdw_ref, bdw_ref, lng_ref, lnb_ref,
               gn_ref, o_ref, zb3, zs, keep):
    t_rows = pc_ref.shape[0]
    wmix = pf_ref.shape[1] // 2

    pc = pc_ref[...].astype(F32)
    ch = pc[:, 2 * wmix:] * pc[:, :wmix]
    hal = pch_ref[...].astype(F32)
    zb3[0:SHORT_HALO, :] = hal[:, 2 * wmix:] * hal[:, :wmix] * keep
    zb3[SHORT_HALO:, :] = ch
    conv = w3_ref[SHORT_K - 1:SHORT_K, :] * ch
    for k in range(SHORT_K - 1):
        off = SHORT_HALO - (SHORT_K - 1) + k
        conv = conv + w3_ref[k:k + 1, :] * zb3[off:off + t_rows, :]
    y_conv = pc[:, wmix:2 * wmix] * conv
    o_ref[:, :wmix] = _rms(y_conv, gn_ref[:, :wmix]).astype(BF16)

    pf = pf_ref[...].astype(F32)
    ph = pfh_ref[...].astype(F32)
    zs[0, 0:CFM_HALO, :] = ph[:, :wmix] * jax.nn.sigmoid(ph[:, wmix:]) * keep
    zs[0, CFM_HALO:, :] = pf[:, :wmix] * jax.nn.sigmoid(pf[:, wmix:])
    span = t_rows + CFM_HALO - 8
    for b in range(1, 8):
        zs[b, 0:span, :] = zs[0, b:b + span, :]

    base = CFM_HALO - (CFM_K - 1)

    for c in range(t_rows // CONV_ROWS):
        r0 = c * CONV_ROWS
        accs = [None] * 4
        for b in range(8):
            ks = [k for k in range(CFM_K) if (base + k) % 8 == b]
            hi = max((base + k) // 8 for k in ks)
            win = zs[b, pl.ds(r0, CONV_ROWS + 8 * hi), :].reshape(CONV_ROWS // 8 + hi, 8, wmix)
            for k in ks:
                a = (base + k) // 8
                term = wdw_ref[k] * win[a:a + CONV_ROWS // 8]
                accs[k % 4] = term if accs[k % 4] is None else accs[k % 4] + term
        acc = (accs[0] + accs[1]) + (accs[2] + accs[3])
        zb3[pl.ds(r0, CONV_ROWS), :] = acc.reshape(CONV_ROWS, wmix) + bdw_ref[...]

    acc = zb3[0:t_rows, :]
    mu = jnp.mean(acc, axis=-1, keepdims=True)
    xc = acc - mu
    var = jnp.mean(xc * xc, axis=-1, keepdims=True)
    y = xc * lax.rsqrt(var + EPS) * lng_ref[...] + lnb_ref[...]
    y = y * jax.nn.sigmoid(y)
    o_ref[:, wmix:] = _rms(y, gn_ref[:, wmix:]).astype(BF16)


def _conv_operands(pc, pf, w3, wdw, bdw, lng, lnb, gn, tt, tile_of_step):
    wmix = pf.shape[1] // 2
    row = lambda g: (tile_of_step(g), 0)

    def halo(rows_):
        return lambda g: (jnp.maximum(tile_of_step(g) * (tt // rows_) - 1, 0), 0)

    vec = lambda a: a.reshape(1, -1)
    operands = (pc, pc, pf, pf, w3, jnp.broadcast_to(wdw[:, None, :], (CFM_K, 8, wmix)), vec(bdw),
                vec(lng), vec(lnb), vec(gn))
    specs = [pl.BlockSpec((tt, 3 * wmix), row),
             pl.BlockSpec((SHORT_HALO, 3 * wmix), halo(SHORT_HALO)),
             pl.BlockSpec((tt, 2 * wmix), row),
             pl.BlockSpec((CFM_HALO, 2 * wmix), halo(CFM_HALO)),
             _resident((SHORT_K, wmix)), _resident((CFM_K, 8, wmix)),
             _resident((1, wmix)), _resident((1, wmix)), _resident((1, wmix)),
             _resident((1, 2 * wmix))]
    scratch = [pltpu.VMEM((tt + SHORT_HALO, wmix), F32), pltpu.VMEM((8, tt + CFM_HALO, wmix), F32)]
    return operands, specs, scratch


def _cmul(ar, ai, br, bi):
    return ar * br - ai * bi, ar * bi + ai * br


def _ssm_discretise(a_re, a_im, log_dt, b_re, b_im, c_re, c_im):
    ng, ns = a_re.shape
    nh = b_re.shape[-1]
    tc = SSM_CHUNK
    dt = jnp.exp(log_dt)[:, None]
    mag = jnp.exp(a_re * dt)
    lr, li = mag * jnp.cos(a_im * dt), mag * jnp.sin(a_im * dt)
    den = a_re * a_re + a_im * a_im
    nr, ni = lr - 1.0, li
    fr = (nr * a_re + ni * a_im) / den
    fi = (ni * a_re - nr * a_im) / den
    bbr = fr[..., None] * b_re - fi[..., None] * b_im
    bbi = fr[..., None] * b_im + fi[..., None] * b_re

    pr, pi = [jnp.ones_like(lr)], [jnp.zeros_like(li)]
    for _ in range(tc):
        r, i = _cmul(pr[-1], pi[-1], lr, li)
        pr.append(r)
        pi.append(i)
    pwr, pwi = jnp.stack(pr), jnp.stack(pi)

    lbr, lbi = _cmul(pwr[:tc, :, :, None], pwi[:tc, :, :, None], bbr[None], bbi[None])
    lbt = jnp.stack([lbr, lbi]).transpose(0, 1, 2, 4, 3).reshape(2, tc, ng * nh, ns)
    ct = jnp.stack([c_re, c_im]).transpose(0, 1, 3, 2).reshape(2, ng * ns, nh)
    clr, cli = _cmul(c_re[None], c_im[None], pwr[1:, :, None, :], pwi[1:, :, None, :])
    clt = jnp.stack([clr, cli]).transpose(0, 1, 2, 4, 3).reshape(2, tc, ng * ns, nh)
    lam8 = jnp.concatenate([pwr[tc].reshape(2, 1, -1), pwi[tc].reshape(2, 1, -1)], axis=2)
    return lbt, ct, clt, lam8


def _split_bf16(x, terms):
    parts = []
    for _ in range(terms):
        part = x.astype(BF16)
        parts.append(part)
        x = x - part.astype(F32)
    return parts


def _ssm_fold_body(lbt_ref, ct_ref, clt_ref, m_ref, w_ref, g_ref):
    tc = lbt_ref.shape[1]
    rows, ns = lbt_ref.shape[2], lbt_ref.shape[3]
    nh = ct_ref.shape[2]
    gh = rows // nh
    srows = gh * ns

    def grid2(shape):
        return (lax.broadcasted_iota(jnp.int32, shape, 0), lax.broadcasted_iota(jnp.int32, shape, 1))

    r, c = grid2((ns, srows))
    spread_p = jnp.where(c % ns == r, 1.0, 0.0).astype(BF16)
    r, c = grid2((nh, rows))
    spread_h = jnp.where(c % nh == r, 1.0, 0.0).astype(BF16)
    r, c = grid2((rows, srows))
    mask_w = r // nh == c // ns
    r, c = grid2((rows, rows))
    mask_m = r // nh == c // nh
    r, c = grid2((srows, rows))
    mask_g = r // ns == c // nh

    def dot(a, b):
        return jnp.dot(a, b, preferred_element_type=F32)

    lb = [[jnp.where(mask_w, sum(dot(t, spread_p) for t in _split_bf16(lbt_ref[ri, d], 3)), 0.0)
           for d in range(tc)] for ri in range(2)]
    for tau in range(tc):
        w_ref[tau * rows:(tau + 1) * rows, :] = jnp.concatenate(
            [lb[0][tc - 1 - tau], lb[1][tc - 1 - tau]], axis=1).astype(BF16)

    c_parts = [_split_bf16(ct_ref[ri], 2) for ri in range(2)]

    def dot_f32(x, y_parts):
        xh, xl = _split_bf16(x, 2)
        return dot(xh, y_parts[0]) + dot(xh, y_parts[1]) + dot(xl, y_parts[0])

    zero = jnp.zeros((rows, rows), BF16)
    blocks = []
    for d in range(tc):
        kd = dot_f32(lb[0][d], c_parts[0]) - dot_f32(lb[1][d], c_parts[1])
        blocks.append(jnp.where(mask_m, dot(kd.astype(BF16), spread_h), 0.0).astype(BF16))
    for tau in range(tc):
        for t in range(tc):
            m_ref[tau * rows:(tau + 1) * rows, t * rows:(t + 1) * rows] = (
                blocks[t - tau] if t >= tau else zero)

    for t in range(tc):
        for ri in range(2):
            blk = jnp.where(mask_g, dot(clt_ref[ri, t].astype(BF16), spread_h), 0.0)
            g_ref[ri * srows:(ri + 1) * srows, t * rows:(t + 1) * rows] = (
                (-blk if ri else blk).astype(BF16))


def _ssm_fold(lbt, ct, clt):
    depth, _, tc, ghn, ns = lbt.shape
    nh = ct.shape[3]
    rows = ghn // 2
    srows = ct.shape[2] // 2
    out = jax.ShapeDtypeStruct((depth, 2, tc * rows, tc * rows), BF16)
    out_spec = pl.BlockSpec((None, None, tc * rows, tc * rows), lambda l, j: (l, j, 0, 0))
    assert 2 * srows == tc * rows
    return pl.pallas_call(
        _ssm_fold_body,
        grid=(depth, 2),
        in_specs=[pl.BlockSpec((None, 2, tc, rows, ns), lambda l, j: (l, 0, 0, j, 0)),
                  pl.BlockSpec((None, 2, srows, nh), lambda l, j: (l, 0, j, 0)),
                  pl.BlockSpec((None, 2, tc, srows, nh), lambda l, j: (l, 0, 0, j, 0))],
        out_specs=[out_spec] * 3,
        out_shape=[out] * 3,
        compiler_params=_cparams("parallel", "parallel"),
        name="ssm_fold",
    )(lbt, ct, clt)


def _ssm_body(u_ref, m_ref, w_ref, g_ref, a_ref, d_ref, wglu_ref, gn_ref, o_ref, state, zbuf, sbuf,
              zrow, ysc, uscr, tok, *, half, group):
    steps = u_ref.shape[0]
    width = u_ref.shape[1] // group
    ch = 2 * half
    nst = a_ref.shape[2] // 2
    assert half == LANES

    for b in range(group):
        part = u_ref[:, b * width:(b + 1) * width].astype(F32)
        for s in range(width // LANES):
            uscr[s, pl.ds(b, steps, stride=group), :] = part[:, s * LANES:(s + 1) * LANES]

    @pl.when(pl.program_id(1) == 0)
    def _():
        state[...] = jnp.zeros(state.shape, F32)

    lanes = [[slice(t * ch + j * half, t * ch + (j + 1) * half) for t in range(SSM_CHUNK)]
             for j in range(2)]
    lhs = [jnp.concatenate([uscr[s.start // LANES] for s in lanes[j]], axis=1).astype(BF16)
           for j in range(2)]
    for j in range(2):
        zbuf[j] = jnp.dot(lhs[j], w_ref[j], preferred_element_type=F32)

    lam = [(jnp.broadcast_to(a_ref[j, :, :nst], (group, nst)),
            jnp.broadcast_to(a_ref[j, :, nst:], (group, nst))) for j in range(2)]

    def chunk(r, carry):
        r0 = pl.multiple_of(r * group, group)
        out = []
        for j in range(2):
            sr, si = carry[2 * j], carry[2 * j + 1]
            sbuf[j, pl.ds(r0, group), :] = jnp.concatenate([sr, si], axis=1)
            z = zbuf[j, pl.ds(r0, group), :]
            ar, ai = lam[j]
            out += [ar * sr - ai * si + z[:, :nst], ar * si + ai * sr + z[:, nst:]]
        return tuple(out)

    final = lax.fori_loop(0, steps, chunk, tuple(state[i] for i in range(4)), unroll=8)
    for i in range(4):
        state[i] = final[i]

    for j in range(2):
        y = (jnp.dot(lhs[j], m_ref[j], preferred_element_type=F32)
             + jnp.dot(sbuf[j].astype(BF16), g_ref[j], preferred_element_type=F32))
        for t, s in enumerate(lanes[j]):
            yy = y[:, t * half:(t + 1) * half] + d_ref[:, s] * uscr[s.start // LANES]
            zrow[:, s] = jax.nn.gelu(yy).astype(BF16)

    for t in range(SSM_CHUNK):
        gl = jnp.dot(zrow[:, t * ch:(t + 1) * ch], wglu_ref[...], preferred_element_type=F32)
        _to_slabs(ysc, _rms(gl[:, :ch] * jax.nn.sigmoid(gl[:, ch:]), gn_ref[...]))
        for b in range(group):
            for s in range(ysc.shape[0]):
                tok[b, s, pl.ds(t, steps, stride=SSM_CHUNK), :] = (
                    ysc[s, pl.ds(b, steps, stride=group), :])
    for b in range(group):
        o_ref[b] = jnp.concatenate([tok[b, s] for s in range(tok.shape[1])], axis=1).astype(BF16)


def _ssm_mixer(u_rows, mats, d_skip, wglu, gn, seq, tile_rows):
    n_groups = u_rows.shape[0]
    width = SSM_CHUNK * wglu.shape[0]
    ch = width // SSM_CHUNK
    m, w, g, a = mats
    group = u_rows.shape[2] // width
    rows_ = tile_rows * group
    d_row = jnp.tile(d_skip, SSM_CHUNK).reshape(1, width)
    nst = a.shape[2] // 2
    out = pl.pallas_call(
        functools.partial(_ssm_body, half=ch // 2, group=group),
        grid=(n_groups, seq // SSM_CHUNK // tile_rows),
        in_specs=[pl.BlockSpec((None, tile_rows, group * width), lambda b, t: (b, t, 0)),
                  _resident(m.shape), _resident(w.shape), _resident(g.shape),
                  _resident(a.shape), _resident((1, width)), _resident(wglu.shape),
                  _resident((1, ch))],
        out_specs=pl.BlockSpec((group, tile_rows * SSM_CHUNK, ch), lambda b, t: (b, t, 0)),
        out_shape=jax.ShapeDtypeStruct((n_groups * group, seq, ch), BF16),
        scratch_shapes=[pltpu.VMEM((4, group, nst), F32),
                        pltpu.VMEM((2, rows_, 2 * nst), F32), pltpu.VMEM((2, rows_, 2 * nst), F32),
                        pltpu.VMEM((rows_, width), BF16),
                        pltpu.VMEM((ch // LANES, rows_, LANES), F32),
                        pltpu.VMEM((width // LANES, rows_, LANES), F32),
                        pltpu.VMEM((group, ch // LANES, tile_rows * SSM_CHUNK, LANES), F32)],
        compiler_params=_cparams("parallel", "arbitrary"),
        name="ssm_mixer",
    )(u_rows, m, w, g, a, d_row, wglu, gn.reshape(1, ch))
    return out.reshape(n_groups * group * seq, ch)


def _attn_body(q_ref, k_ref, v_ref, o_ref, st_ref, sbuf, pbuf, *, n_heads, qb):
    blk = ATT_BLK
    wd = q_ref.shape[1]
    nblk = q_ref.shape[0] // blk

    qi = lax.broadcasted_iota(jnp.int32, (blk, 2 * blk), 0)
    kj = lax.broadcasted_iota(jnp.int32, (blk, 2 * blk), 1)
    rel = blk + qi - kj
    bias_any = jnp.where((rel >= 0) & (rel <= blk), 0.0, NEG_BIG)
    bias_first = jnp.where(kj <= qi, 0.0, NEG_BIG)
    head_of_lane = lax.broadcasted_iota(jnp.int32, (1, wd), 1) // HEAD_DIM
    st_lane = lax.broadcasted_iota(jnp.int32, (1, st_ref.shape[1]), 1)

    def rows_of(f):
        return f * blk if isinstance(f, int) else pl.multiple_of(f * blk, blk)

    def window_of(f):
        first = (f % qb) == 0
        if isinstance(f, int):
            return (f if first else f - 1) * blk, first
        return pl.multiple_of((f - jnp.where(first, 0, 1)) * blk, blk), first

    def scores(f, slot):
        q = q_ref[pl.ds(rows_of(f), blk), :]
        qs = jnp.concatenate([jnp.where(head_of_lane == h, q, jnp.zeros_like(q))
                              for h in range(n_heads)], axis=0)
        sbuf[slot] = lax.dot_general(qs, k_ref[pl.ds(window_of(f)[0], 2 * blk), :],
                                     (((1,), (1,)), ((), ())), preferred_element_type=F32)

    def softmax(f, slot):
        first = window_of(f)[1]
        bias = (bias_first if first else bias_any) if isinstance(f, int) else jnp.where(
            first, bias_first, bias_any)
        stats = jnp.zeros((blk, st_ref.shape[1]), F32)
        for h in range(n_heads):
            s = sbuf[slot, h * blk:(h + 1) * blk, :] + bias
            m = jnp.max(s, axis=-1, keepdims=True)
            p = jnp.exp(s - m)
            pbuf[slot, h * blk:(h + 1) * blk, :] = p.astype(BF16)
            stats = jnp.where(st_lane == h, m, stats)
            stats = jnp.where(st_lane == n_heads + h, jnp.sum(p, axis=-1, keepdims=True), stats)
        st_ref[pl.ds(rows_of(f), blk), :] = stats

    def values(f, slot):
        o_all = jnp.dot(pbuf[slot], v_ref[pl.ds(window_of(f)[0], 2 * blk), :],
                        preferred_element_type=F32)
        o = o_all[0:blk]
        for h in range(1, n_heads):
            o = jnp.where(head_of_lane == h, o_all[h * blk:(h + 1) * blk], o)
        o_ref[pl.ds(rows_of(f), blk), :] = o.astype(BF16)

    def step(f, slot):
        scores(f, slot)
        softmax(f - 1, 1 - slot)
        values(f - 2, slot)

    scores(0, 0)
    scores(1, 1)
    softmax(0, 0)

    def pair(it, carry):
        step(2 * it + 2, 0)
        step(2 * it + 3, 1)
        return carry

    lax.fori_loop(0, (nblk - 2) // 2, pair, 0)
    softmax(nblk - 1, 1)
    values(nblk - 2, 0)
    values(nblk - 1, 1)


ST_LANES = LANES


def _attention(q, k, v):
    batch, dilation, rows_, wd = q.shape
    seq = dilation * rows_
    n_heads = wd // HEAD_DIM
    qb = rows_ // ATT_BLK
    assert qb >= 2 and qb & (qb - 1) == 0
    flat = lambda a: a.reshape(batch, seq, wd)
    spec = pl.BlockSpec((None, seq, wd), lambda b: (b, 0, 0))
    o, st = pl.pallas_call(
        functools.partial(_attn_body, n_heads=n_heads, qb=qb),
        grid=(batch,),
        in_specs=[spec, spec, spec],
        out_specs=[spec, pl.BlockSpec((None, seq, ST_LANES), lambda b: (b, 0, 0))],
        out_shape=[jax.ShapeDtypeStruct((batch, seq, wd), BF16),
                   jax.ShapeDtypeStruct((batch, seq, ST_LANES), F32)],
        scratch_shapes=[pltpu.VMEM((2, n_heads * ATT_BLK, 2 * ATT_BLK), F32),
                        pltpu.VMEM((2, n_heads * ATT_BLK, 2 * ATT_BLK), BF16)],
        compiler_params=_cparams("parallel"),
        name=f"attn_d{dilation}",
    )(flat(q), flat(k), flat(v))
    return (o.reshape(batch, dilation, rows_, wd), st.reshape(batch, dilation, rows_, ST_LANES))


def _token_order(block, d, bufs):
    slabs, tm, _ = bufs[0].shape
    stage = 0
    while d > 1:
        f = 4 if d % 4 == 0 else d
        nxt, rows, dst = d // f, tm // d, bufs[stage % 2]
        for r1 in range(nxt):
            for a in range(f):
                blk = block(r1 + nxt * a)
                for s in range(slabs):
                    dst[s, pl.ds(r1 * (tm // nxt) + a, rows, stride=f), :] = (
                        blk[:, s * LANES:(s + 1) * LANES])

        def block(r, dst=dst, size=tm // nxt):
            return jnp.concatenate([dst[s, pl.ds(r * size, size), :] for s in range(slabs)], axis=1)

        d, stage = nxt, stage + 1
    return block(0)


def _att_mix(o_refs, s_refs, gn_ref, oscr, sscr, dilations):
    tm, wd = oscr.shape[3], oscr.shape[2] * LANES
    n_heads = wd // HEAD_DIM
    outs, lses = [], []
    for p, d in enumerate(dilations):
        outs.append(_token_order(lambda r, p=p: o_refs[p][r].astype(F32), d,
                                 (oscr.at[p, 0], oscr.at[p, 1])))
        lses.append(_token_order(lambda r, p=p: s_refs[p][r], d, (sscr.at[p, 0], sscr.at[p, 1])))
    m = functools.reduce(jnp.maximum, lses)
    head_of_lane = lax.broadcasted_iota(jnp.int32, (1, wd), 1) // HEAD_DIM

    def widen(x, lane0):
        wide = jnp.zeros((tm, wd), F32)
        for h in range(n_heads):
            wide = jnp.where(head_of_lane == h, x[:, lane0 + h:lane0 + h + 1], wide)
        return wide

    num = jnp.zeros((tm, wd), F32)
    den = jnp.zeros((tm, wd), F32)
    for st, o in zip(lses, outs):
        e = widen(jnp.exp(st - m), 0)
        num = num + e * o
        den = den + e * widen(st, n_heads)
    return _rms(num / den, gn_ref[...]).astype(BF16)


FF_CHUNK = 256


def _out_ffn_body(x_ref, ys_ref, *rest, final_norm, dilations, tiles, per_seq):
    npat = len(dilations)
    conv_in, rest = rest[:10], rest[10:]
    o_refs, s_refs = rest[:npat], rest[npat:2 * npat]
    (gna_ref, wo_ref, gf_ref, wg_ref, wu_ref, wd_ref, gl_ref, o_ref, act, ya, oscr, sscr, ycc, zb3,
     zs) = rest[2 * npat:]
    g = pl.program_id(0)
    first_of_seq = jnp.minimum(g, tiles - 1) % per_seq == 0

    @pl.when(g == 0)
    def _():
        ya[1] = jnp.zeros(ya.shape[1:], BF16)
        ycc[1] = jnp.zeros(ycc.shape[1:], BF16)

    ya_prev = ya[(g + 1) % 2]
    ycc_prev = ycc[(g + 1) % 2]
    ya[g % 2] = _att_mix(o_refs, s_refs, gna_ref, oscr, sscr, dilations)
    _conv_tile(*conv_in, ycc.at[g % 2], zb3, zs, jnp.where(first_of_seq, 0.0, 1.0))

    mixed = jnp.concatenate([ycc_prev, ys_ref[...], ya_prev], axis=1)
    x1 = x_ref[...] + jnp.dot(mixed, wo_ref[...], preferred_element_type=F32)
    o_ref[...] = x1
    h = _rms(x1, gf_ref[...]).astype(BF16)
    dff = wg_ref.shape[1]
    for c in range(dff // FF_CHUNK):
        sl = slice(c * FF_CHUNK, (c + 1) * FF_CHUNK)
        g = jnp.dot(h, wg_ref[:, sl], preferred_element_type=F32)
        u = jnp.dot(h, wu_ref[:, sl], preferred_element_type=F32)
        act[:, sl] = (g * jax.nn.sigmoid(g) * u).astype(BF16)
    x2 = o_ref[...] + jnp.dot(act[...], wd_ref[...], preferred_element_type=F32)
    if final_norm:
        x2 = _rms(x2, gl_ref[...])
    o_ref[...] = x2


def _out_ffn(x, conv_args, ys, att_outs, att_stats, gna, wo, gf, wg, wu, wd, gl, final_norm, tm):
    n, d = x.shape
    dff = wg.shape[1]
    batch, _, _, wa = att_outs[0].shape
    seq = n // batch
    dilations = tuple(o.shape[1] for o in att_outs)
    per_seq = seq // tm
    tiles = n // tm
    row = lambda g: (jnp.maximum(g - 1, 0), 0)
    ahead = lambda g: jnp.minimum(g, tiles - 1)
    conv_ops, conv_specs, conv_scratch = _conv_operands(*conv_args, tm, ahead)
    wcc = conv_ops[-1].shape[1]

    def res_spec(dl, width):
        return pl.BlockSpec((None, dl, tm // dl, width),
                            lambda g: (ahead(g) // per_seq, 0, ahead(g) % per_seq, 0))

    return pl.pallas_call(
        functools.partial(_out_ffn_body, final_norm=final_norm, dilations=dilations, tiles=tiles,
                          per_seq=per_seq),
        grid=(tiles + 1,),
        in_specs=[pl.BlockSpec((tm, d), row), pl.BlockSpec((tm, ys.shape[1]), row)] + conv_specs
        + [res_spec(dl, wa) for dl in dilations] + [res_spec(dl, ST_LANES) for dl in dilations]
        + [_resident((1, wa)), _resident(wo.shape), _resident((1, d)), _resident(wg.shape),
           _resident(wu.shape), _resident(wd.shape), _resident((1, d))],
        out_specs=pl.BlockSpec((tm, d), row),
        out_shape=jax.ShapeDtypeStruct((n, d), F32),
        scratch_shapes=[pltpu.VMEM((tm, dff), BF16), pltpu.VMEM((2, tm, wa), BF16),
                        pltpu.VMEM((len(dilations), 2, wa // LANES, tm, LANES), F32),
                        pltpu.VMEM((len(dilations), 2, ST_LANES // LANES, tm, LANES), F32),
                        pltpu.VMEM((2, tm, wcc), BF16)] + conv_scratch,
        compiler_params=_cparams("arbitrary"),
        name="out_ffn",
    )(x, ys, *conv_ops, *att_outs, *att_stats, gna.reshape(1, wa), wo, gf.reshape(1, d), wg, wu,
      wd, gl.reshape(1, d))


def _rope_tables(seq, wd):
    half = HEAD_DIM // 2
    inv = ROPE_THETA ** (-jnp.arange(0, HEAD_DIM, 2, dtype=F32) / HEAD_DIM)
    ang = jnp.arange(seq, dtype=F32)[:, None] * inv[None, :]
    cos, sin = jnp.cos(ang), jnp.sin(ang)
    reps = wd // HEAD_DIM
    cos_t = jnp.tile(jnp.concatenate([cos, cos], axis=1), (1, reps))
    sin_t = jnp.tile(jnp.concatenate([-sin, sin], axis=1), (1, reps))
    scale = HEAD_DIM ** -0.5
    return cos_t * scale, sin_t * scale, cos_t, sin_t


def kernel(x, norm_mix_g, w_in, conv3_w, cfm_dw_w, cfm_dw_b, cfm_ln_g, cfm_ln_b, s5_a_re, s5_a_im, s5_log_dt, s5_b_re, s5_b_im, s5_c_re, s5_c_im, s5_d, s5_glu_w, grp_norm_g, w_out, norm_ffn_g, w_gate, w_up, w_down, final_norm_g):
    batch, seq, d_model = x.shape
    depth = w_in.shape[0]
    wmix = w_in.shape[2] // 9
    n = batch * seq
    assert wmix == 2 * 8 * SSM_CH and all(w // d == ATT_BLK for w, d in DILATED_CFG)
    assert seq % (DILATED_CFG[-1][1] * ATT_BLK) == 0 and seq % 512 == 0
    tm = 512
    group = math.gcd(batch, 8)
    rope_tabs = _rope_tables(seq, wmix)

    dilations = tuple(dil for _, dil in DILATED_CFG)
    npat = len(dilations)

    w_in, s5_glu_w, w_out, w_gate, w_up, w_down = (
        a.astype(BF16) for a in (w_in, s5_glu_w, w_out, w_gate, w_up, w_down))
    lbt, ct, clt, lam8 = jax.vmap(_ssm_discretise)(s5_a_re, s5_a_im, s5_log_dt, s5_b_re, s5_b_im,
                                                   s5_c_re, s5_c_im)
    ssm_mats = (*_ssm_fold(lbt, ct, clt), lam8)

    xf = x.reshape(n, d_model)
    for layer in range(depth):
        pc, pf, u, *qkv = _in_proj(xf, norm_mix_g[layer], w_in[layer], rope_tabs,
                                   batch, seq, 2 * tm, dilations, group)
        gn = grp_norm_g[layer]
        conv_args = (pc, pf, conv3_w[layer], cfm_dw_w[layer], cfm_dw_b[layer], cfm_ln_g[layer],
                     cfm_ln_b[layer], gn[:2 * wmix])
        ys = _ssm_mixer(u, [a[layer] for a in ssm_mats], s5_d[layer], s5_glu_w[layer],
                        gn[2 * wmix:3 * wmix], seq, tm // SSM_CHUNK)
        outs, stats = zip(*[_attention(qkv[p], qkv[npat + p], qkv[2 * npat + p])
                            for p in range(npat)])
        xf = _out_ffn(xf, conv_args, ys, outs, stats, gn[3 * wmix:], w_out[layer],
                      norm_ffn_g[layer],
                      w_gate[layer], w_up[layer], w_down[layer], final_norm_g,
                      layer == depth - 1, tm)
    return xf.reshape(batch, seq, d_model)
```

```python
import functools
import math

import jax
import jax.numpy as jnp
from jax import lax
from jax.experimental import pallas as pl
from jax.experimental.pallas import tpu as pltpu

F32 = jnp.float32
BF16 = jnp.bfloat16

EPS = 1e-6
HEAD_DIM = 64
SHORT_K = 3
CFM_K = 31
SSM_CH = 16
SSM_STATE = 64
SSM_CHUNK = 8
DILATED_CFG = ((128, 1), (512, 4), (2048, 16))
ROPE_THETA = 10000.0
ATT_BLK = 128
NEG_BIG = -1e30
VMEM_LIMIT = 56 * 1024 * 1024


def _cparams(*sem):
    return pltpu.CompilerParams(dimension_semantics=sem, vmem_limit_bytes=VMEM_LIMIT)


def _rms(x, g):
    return x * lax.rsqrt(jnp.mean(x * x, axis=-1, keepdims=True) + EPS) * g


def _resident(shape):
    nd = len(shape)
    return pl.BlockSpec(shape, lambda *_: (0,) * nd, pipeline_mode=pl.Buffered(1))


LANES = 128


def _to_slabs(scr, val):
    for s in range(scr.shape[0]):
        scr[s] = val[:, s * LANES:(s + 1) * LANES]


def _rows_strided(scr, start, count, stride):
    return jnp.concatenate([scr[s, pl.ds(start, count, stride=stride), :]
                            for s in range(scr.shape[0])], axis=1)


def _in_proj_body(x_ref, g_ref, w_ref, cq_ref, sq_ref, ck_ref, sk_ref, pc_ref, pf_ref, u_ref,
                  *rest, wmix, dilations):
    qkv_refs, scr = rest[:-1], rest[-1]
    tm = x_ref.shape[0]
    h = _rms(x_ref[...], g_ref[...]).astype(BF16)

    def proj(lo, width):
        return jnp.dot(h, w_ref[:, lo:lo + width], preferred_element_type=F32)

    _to_slabs(scr.at[3, 0], proj(5 * wmix, wmix))
    for t in range(SSM_CHUNK):
        u_ref[:, t * wmix:(t + 1) * wmix] = _rows_strided(
            scr.at[3, 0], t, tm // SSM_CHUNK, SSM_CHUNK).astype(BF16)

    lane = lax.broadcasted_iota(jnp.int32, (1, wmix), 1)
    first_half = (lane % HEAD_DIM) < (HEAD_DIM // 2)

    def rope(t, c_ref, s_ref):
        partner = jnp.where(first_half, pltpu.roll(t, wmix - HEAD_DIM // 2, 1),
                            pltpu.roll(t, HEAD_DIM // 2, 1))
        return t * c_ref[...] + partner * s_ref[...]

    vals = (rope(proj(6 * wmix, wmix), cq_ref, sq_ref), rope(proj(7 * wmix, wmix), ck_ref, sk_ref),
            proj(8 * wmix, wmix))
    for i, val in enumerate(vals):
        outs = qkv_refs[i * len(dilations):(i + 1) * len(dilations)]
        prev, stage = 1, 0
        for n_d, (d, o_ref) in enumerate(zip(dilations, outs)):
            if d == 1:
                o_ref[0] = val.astype(BF16)
                continue
            src, dst = scr.at[i, stage % 2], scr.at[i, 1 - stage % 2]
            if stage == 0:
                _to_slabs(src, val)
            ratio, rows = d // prev, tm // d
            for r in range(prev):
                for a in range(ratio):
                    part = _rows_strided(src, r * (tm // prev) + a, rows, ratio)
                    o_ref[r + prev * a] = part.astype(BF16)
                    if n_d + 1 < len(dilations):
                        _to_slabs(dst.at[:, pl.ds((r + prev * a) * rows, rows)], part)
            prev, stage = d, stage + 1

    pc_ref[...] = proj(0, 3 * wmix).astype(BF16)
    pf_ref[...] = proj(3 * wmix, 2 * wmix).astype(BF16)


def _in_proj(x, g, w, rope_tabs, batch, seq, tm, dilations, group):
    n, d = x.shape
    nin = w.shape[1]
    wmix = nin // 9
    per_seq = seq // tm
    row = lambda i: (i, 0)
    tab = lambda i: (i % per_seq, 0)
    flat = [(n, 3 * wmix), (n, 2 * wmix)]
    flat_blocks = [(tm, 3 * wmix), (tm, 2 * wmix)]
    uw = SSM_CHUNK * wmix
    u_shape = (batch // group, seq // SSM_CHUNK, group * uw)
    u_spec = pl.BlockSpec((None, tm // SSM_CHUNK, uw),
                          lambda i: (i // per_seq // group, i % per_seq, i // per_seq % group))
    res_shapes = [(batch, dl, seq // dl, wmix) for dl in dilations] * 3
    res_specs = [pl.BlockSpec((None, dl, tm // dl, wmix),
                              lambda i: (i // per_seq, 0, i % per_seq, 0)) for dl in dilations] * 3
    outs = pl.pallas_call(
        functools.partial(_in_proj_body, wmix=wmix, dilations=dilations),
        grid=(n // tm,),
        in_specs=[pl.BlockSpec((tm, d), row), _resident((1, d)), _resident((d, nin))]
        + [pl.BlockSpec((tm, wmix), tab)] * 4,
        out_specs=[pl.BlockSpec(b, row) for b in flat_blocks] + [u_spec] + res_specs,
        out_shape=[jax.ShapeDtypeStruct(s, BF16) for s in flat + [u_shape] + res_shapes],
        scratch_shapes=[pltpu.VMEM((4, 2, wmix // LANES, tm, LANES), F32)],
        compiler_params=_cparams("parallel"),
        name="in_proj",
    )(x, g.reshape(1, d), w, *rope_tabs)
    return outs


CONV_ROWS = 32
CFM_HALO = 32
SHORT_HALO = 16


def _conv_tile(pc_ref, pch_ref, pf_ref, pfh_ref, w3_ref, wdw_ref, bdw_ref, lng_ref, lnb_ref,
               gn_ref, o_ref, zb3, zs, keep):
    t_rows = pc_ref.shape[0]
    wmix = pf_ref.shape[1] // 2

    pc = pc_ref[...].astype(F32)
    ch = pc[:, 2 * wmix:] * pc[:, :wmix]
    hal = pch_ref[...].astype(F32)
    zb3[0:SHORT_HALO, :] = hal[:, 2 * wmix:] * hal[:, :wmix] * keep
    zb3[SHORT_HALO:, :] = ch
    conv = w3_ref[SHORT_K - 1:SHORT_K, :] * ch
    for k in range(SHORT_K - 1):
        off = SHORT_HALO - (SHORT_K - 1) + k
        conv = conv + w3_ref[k:k + 1, :] * zb3[off:off + t_rows, :]
    y_conv = pc[:, wmix:2 * wmix] * conv
    o_ref[:, :wmix] = _rms(y_conv, gn_ref[:, :wmix]).astype(BF16)

    pf = pf_ref[...].astype(F32)
    ph = pfh_ref[...].astype(F32)
    zs[0, 0:CFM_HALO, :] = ph[:, :wmix] * jax.nn.sigmoid(ph[:, wmix:]) * keep
    zs[0, CFM_HALO:, :] = pf[:, :wmix] * jax.nn.sigmoid(pf[:, wmix:])
    span = t_rows + CFM_HALO - 8
    for b in range(1, 8):
        zs[b, 0:span, :] = zs[0, b:b + span, :]

    base = CFM_HALO - (CFM_K - 1)

    for c in range(t_rows // CONV_ROWS):
        r0 = c * CONV_ROWS
        accs = [None] * 4
        for b in range(8):
            ks = [k for k in range(CFM_K) if (base + k) % 8 == b]
            hi = max((base + k) // 8 for k in ks)
            win = zs[b, pl.ds(r0, CONV_ROWS + 8 * hi), :].reshape(CONV_ROWS // 8 + hi, 8, wmix)
            for k in ks:
                a = (base + k) // 8
                term = wdw_ref[k] * win[a:a + CONV_ROWS // 8]
                accs[k % 4] = term if accs[k % 4] is None else accs[k % 4] + term
        acc = (accs[0] + accs[1]) + (accs[2] + accs[3])
        zb3[pl.ds(r0, CONV_ROWS), :] = acc.reshape(CONV_ROWS, wmix) + bdw_ref[...]

    acc = zb3[0:t_rows, :]
    mu = jnp.mean(acc, axis=-1, keepdims=True)
    xc = acc - mu
    var = jnp.mean(xc * xc, axis=-1, keepdims=True)
    y = xc * lax.rsqrt(var + EPS) * lng_ref[...] + lnb_ref[...]
    y = y * jax.nn.sigmoid(y)
    o_ref[:, wmix:] = _rms(y, gn_ref[:, wmix:]).astype(BF16)


def _conv_operands(pc, pf, w3, wdw, bdw, lng, lnb, gn, tt, tile_of_step):
    wmix = pf.shape[1] // 2
    row = lambda g: (tile_of_step(g), 0)

    def halo(rows_):
        return lambda g: (jnp.maximum(tile_of_step(g) * (tt // rows_) - 1, 0), 0)

    vec = lambda a: a.reshape(1, -1)
    operands = (pc, pc, pf, pf, w3, jnp.broadcast_to(wdw[:, None, :], (CFM_K, 8, wmix)), vec(bdw),
                vec(lng), vec(lnb), vec(gn))
    specs = [pl.BlockSpec((tt, 3 * wmix), row),
             pl.BlockSpec((SHORT_HALO, 3 * wmix), halo(SHORT_HALO)),
             pl.BlockSpec((tt, 2 * wmix), row),
             pl.BlockSpec((CFM_HALO, 2 * wmix), halo(CFM_HALO)),
             _resident((SHORT_K, wmix)), _resident((CFM_K, 8, wmix)),
             _resident((1, wmix)), _resident((1, wmix)), _resident((1, wmix)),
             _resident((1, 2 * wmix))]
    scratch = [pltpu.VMEM((tt + SHORT_HALO, wmix), F32), pltpu.VMEM((8, tt + CFM_HALO, wmix), F32)]
    return operands, specs, scratch


def _cmul(ar, ai, br, bi):
    return ar * br - ai * bi, ar * bi + ai * br


def _ssm_discretise(a_re, a_im, log_dt, b_re, b_im, c_re, c_im):
    ng, ns = a_re.shape
    nh = b_re.shape[-1]
    tc = SSM_CHUNK
    dt = jnp.exp(log_dt)[:, None]
    mag = jnp.exp(a_re * dt)
    lr, li = mag * jnp.cos(a_im * dt), mag * jnp.sin(a_im * dt)
    den = a_re * a_re + a_im * a_im
    nr, ni = lr - 1.0, li
    fr = (nr * a_re + ni * a_im) / den
    fi = (ni * a_re - nr * a_im) / den
    bbr = fr[..., None] * b_re - fi[..., None] * b_im
    bbi = fr[..., None] * b_im + fi[..., None] * b_re

    pr, pi = [jnp.ones_like(lr)], [jnp.zeros_like(li)]
    for _ in range(tc):
        r, i = _cmul(pr[-1], pi[-1], lr, li)
        pr.append(r)
        pi.append(i)
    pwr, pwi = jnp.stack(pr), jnp.stack(pi)

    lbr, lbi = _cmul(pwr[:tc, :, :, None], pwi[:tc, :, :, None], bbr[None], bbi[None])
    lbt = jnp.stack([lbr, lbi]).transpose(0, 1, 2, 4, 3).reshape(2, tc, ng * nh, ns)
    ct = jnp.stack([c_re, c_im]).transpose(0, 1, 3, 2).reshape(2, ng * ns, nh)
    clr, cli = _cmul(c_re[None], c_im[None], pwr[1:, :, None, :], pwi[1:, :, None, :])
    clt = jnp.stack([clr, cli]).transpose(0, 1, 2, 4, 3).reshape(2, tc, ng * ns, nh)
    lam8 = jnp.concatenate([pwr[tc].reshape(2, 1, -1), pwi[tc].reshape(2, 1, -1)], axis=2)
    return lbt, ct, clt, lam8


def _split_bf16(x, terms):
    parts = []
    for _ in range(terms):
        part = x.astype(BF16)
        parts.append(part)
        x = x - part.astype(F32)
    return parts


def _ssm_fold_body(lbt_ref, ct_ref, clt_ref, m_ref, w_ref, g_ref):
    tc = lbt_ref.shape[1]
    rows, ns = lbt_ref.shape[2], lbt_ref.shape[3]
    nh = ct_ref.shape[2]
    gh = rows // nh
    srows = gh * ns

    def grid2(shape):
        return (lax.broadcasted_iota(jnp.int32, shape, 0), lax.broadcasted_iota(jnp.int32, shape, 1))

    r, c = grid2((ns, srows))
    spread_p = jnp.where(c % ns == r, 1.0, 0.0).astype(BF16)
    r, c = grid2((nh, rows))
    spread_h = jnp.where(c % nh == r, 1.0, 0.0).astype(BF16)
    r, c = grid2((rows, srows))
    mask_w = r // nh == c // ns
    r, c = grid2((rows, rows))
    mask_m = r // nh == c // nh
    r, c = grid2((srows, rows))
    mask_g = r // ns == c // nh

    def dot(a, b):
        return jnp.dot(a, b, preferred_element_type=F32)

    lb = [[jnp.where(mask_w, sum(dot(t, spread_p) for t in _split_bf16(lbt_ref[ri, d], 3)), 0.0)
           for d in range(tc)] for ri in range(2)]
    for tau in range(tc):
        w_ref[tau * rows:(tau + 1) * rows, :] = jnp.concatenate(
            [lb[0][tc - 1 - tau], lb[1][tc - 1 - tau]], axis=1).astype(BF16)

    c_parts = [_split_bf16(ct_ref[ri], 2) for ri in range(2)]

    def dot_f32(x, y_parts):
        xh, xl = _split_bf16(x, 2)
        return dot(xh, y_parts[0]) + dot(xh, y_parts[1]) + dot(xl, y_parts[0])

    zero = jnp.zeros((rows, rows), BF16)
    blocks = []
    for d in range(tc):
        kd = dot_f32(lb[0][d], c_parts[0]) - dot_f32(lb[1][d], c_parts[1])
        blocks.append(jnp.where(mask_m, dot(kd.astype(BF16), spread_h), 0.0).astype(BF16))
    for tau in range(tc):
        for t in range(tc):
            m_ref[tau * rows:(tau + 1) * rows, t * rows:(t + 1) * rows] = (
                blocks[t - tau] if t >= tau else zero)

    for t in range(tc):
        for ri in range(2):
            blk = jnp.where(mask_g, dot(clt_ref[ri, t].astype(BF16), spread_h), 0.0)
            g_ref[ri * srows:(ri + 1) * srows, t * rows:(t + 1) * rows] = (
                (-blk if ri else blk).astype(BF16))


def _ssm_fold(lbt, ct, clt):
    depth, _, tc, ghn, ns = lbt.shape
    nh = ct.shape[3]
    rows = ghn // 2
    srows = ct.shape[2] // 2
    out = jax.ShapeDtypeStruct((depth, 2, tc * rows, tc * rows), BF16)
    out_spec = pl.BlockSpec((None, None, tc * rows, tc * rows), lambda l, j: (l, j, 0, 0))
    assert 2 * srows == tc * rows
    return pl.pallas_call(
        _ssm_fold_body,
        grid=(depth, 2),
        in_specs=[pl.BlockSpec((None, 2, tc, rows, ns), lambda l, j: (l, 0, 0, j, 0)),
                  pl.BlockSpec((None, 2, srows, nh), lambda l, j: (l, 0, j, 0)),
                  pl.BlockSpec((None, 2, tc, srows, nh), lambda l, j: (l, 0, 0, j, 0))],
        out_specs=[out_spec] * 3,
        out_shape=[out] * 3,
        compiler_params=_cparams("parallel", "parallel"),
        name="ssm_fold",
    )(lbt, ct, clt)


def _ssm_body(u_ref, m_ref, w_ref, g_ref, a_ref, d_ref, wglu_ref, gn_ref, o_ref, state, zbuf, sbuf,
              zrow, ysc, uscr, tok, *, half, group):
    steps = u_ref.shape[0]
    width = u_ref.shape[1] // group
    ch = 2 * half
    nst = a_ref.shape[2] // 2
    assert half == LANES

    for b in range(group):
        part = u_ref[:, b * width:(b + 1) * width].astype(F32)
        for s in range(width // LANES):
            uscr[s, pl.ds(b, steps, stride=group), :] = part[:, s * LANES:(s + 1) * LANES]

    @pl.when(pl.program_id(1) == 0)
    def _():
        state[...] = jnp.zeros(state.shape, F32)

    lanes = [[slice(t * ch + j * half, t * ch + (j + 1) * half) for t in range(SSM_CHUNK)]
             for j in range(2)]
    lhs = [jnp.concatenate([uscr[s.start // LANES] for s in lanes[j]], axis=1).astype(BF16)
           for j in range(2)]
    for j in range(2):
        zbuf[j] = jnp.dot(lhs[j], w_ref[j], preferred_element_type=F32)

    lam = [(jnp.broadcast_to(a_ref[j, :, :nst], (group, nst)),
            jnp.broadcast_to(a_ref[j, :, nst:], (group, nst))) for j in range(2)]

    def chunk(r, carry):
        r0 = pl.multiple_of(r * group, group)
        out = []
        for j in range(2):
            sr, si = carry[2 * j], carry[2 * j + 1]
            sbuf[j, pl.ds(r0, group), :] = jnp.concatenate([sr, si], axis=1)
            z = zbuf[j, pl.ds(r0, group), :]
            ar, ai = lam[j]
            out += [ar * sr - ai * si + z[:, :nst], ar * si + ai * sr + z[:, nst:]]
        return tuple(out)

    final = lax.fori_loop(0, steps, chunk, tuple(state[i] for i in range(4)), unroll=8)
    for i in range(4):
        state[i] = final[i]

    for j in range(2):
        y = (jnp.dot(lhs[j], m_ref[j], preferred_element_type=F32)
             + jnp.dot(sbuf[j].astype(BF16), g_ref[j], preferred_element_type=F32))
        for t, s in enumerate(lanes[j]):
            yy = y[:, t * half:(t + 1) * half] + d_ref[:, s] * uscr[s.start // LANES]
            zrow[:, s] = jax.nn.gelu(yy).astype(BF16)

    for t in range(SSM_CHUNK):
        gl = jnp.dot(zrow[:, t * ch:(t + 1) * ch], wglu_ref[...], preferred_element_type=F32)
        _to_slabs(ysc, _rms(gl[:, :ch] * jax.nn.sigmoid(gl[:, ch:]), gn_ref[...]))
        for b in range(group):
            for s in range(ysc.shape[0]):
                tok[b, s, pl.ds(t, steps, stride=SSM_CHUNK), :] = (
                    ysc[s, pl.ds(b, steps, stride=group), :])
    for b in range(group):
        o_ref[b] = jnp.concatenate([tok[b, s] for s in range(tok.shape[1])], axis=1).astype(BF16)


def _ssm_mixer(u_rows, mats, d_skip, wglu, gn, seq, tile_rows):
    n_groups = u_rows.shape[0]
    width = SSM_CHUNK * wglu.shape[0]
    ch = width // SSM_CHUNK
    m, w, g, a = mats
    group = u_rows.shape[2] // width
    rows_ = tile_rows * group
    d_row = jnp.tile(d_skip, SSM_CHUNK).reshape(1, width)
    nst = a.shape[2] // 2
    out = pl.pallas_call(
        functools.partial(_ssm_body, half=ch // 2, group=group),
        grid=(n_groups, seq // SSM_CHUNK // tile_rows),
        in_specs=[pl.BlockSpec((None, tile_rows, group * width), lambda b, t: (b, t, 0)),
                  _resident(m.shape), _resident(w.shape), _resident(g.shape),
                  _resident(a.shape), _resident((1, width)), _resident(wglu.shape),
                  _resident((1, ch))],
        out_specs=pl.BlockSpec((group, tile_rows * SSM_CHUNK, ch), lambda b, t: (b, t, 0)),
        out_shape=jax.ShapeDtypeStruct((n_groups * group, seq, ch), BF16),
        scratch_shapes=[pltpu.VMEM((4, group, nst), F32),
                        pltpu.VMEM((2, rows_, 2 * nst), F32), pltpu.VMEM((2, rows_, 2 * nst), F32),
                        pltpu.VMEM((rows_, width), BF16),
                        pltpu.VMEM((ch // LANES, rows_, LANES), F32),
                        pltpu.VMEM((width // LANES, rows_, LANES), F32),
                        pltpu.VMEM((group, ch // LANES, tile_rows * SSM_CHUNK, LANES), F32)],
        compiler_params=_cparams("parallel", "arbitrary"),
        name="ssm_mixer",
    )(u_rows, m, w, g, a, d_row, wglu, gn.reshape(1, ch))
    return out.reshape(n_groups * group * seq, ch)


def _attn_body(q_ref, k_ref, v_ref, o_ref, st_ref, sbuf, pbuf, *, n_heads, qb):
    blk = ATT_BLK
    wd = q_ref.shape[1]
    nblk = q_ref.shape[0] // blk

    qi = lax.broadcasted_iota(jnp.int32, (blk, 2 * blk), 0)
    kj = lax.broadcasted_iota(jnp.int32, (blk, 2 * blk), 1)
    rel = blk + qi - kj
    bias_any = jnp.where((rel >= 0) & (rel <= blk), 0.0, NEG_BIG)
    bias_first = jnp.where(kj <= qi, 0.0, NEG_BIG)
    head_of_lane = lax.broadcasted_iota(jnp.int32, (1, wd), 1) // HEAD_DIM
    st_lane = lax.broadcasted_iota(jnp.int32, (1, st_ref.shape[1]), 1)

    def rows_of(f):
        return f * blk if isinstance(f, int) else pl.multiple_of(f * blk, blk)

    def window_of(f):
        first = (f % qb) == 0
        if isinstance(f, int):
            return (f if first else f - 1) * blk, first
        return pl.multiple_of((f - jnp.where(first, 0, 1)) * blk, blk), first

    def scores(f, slot):
        q = q_ref[pl.ds(rows_of(f), blk), :]
        qs = jnp.concatenate([jnp.where(head_of_lane == h, q, jnp.zeros_like(q))
                              for h in range(n_heads)], axis=0)
        sbuf[slot] = lax.dot_general(qs, k_ref[pl.ds(window_of(f)[0], 2 * blk), :],
                                     (((1,), (1,)), ((), ())), preferred_element_type=F32)

    def softmax(f, slot):
        first = window_of(f)[1]
        bias = (bias_first if first else bias_any) if isinstance(f, int) else jnp.where(
            first, bias_first, bias_any)
        stats = jnp.zeros((blk, st_ref.shape[1]), F32)
        for h in range(n_heads):
            s = sbuf[slot, h * blk:(h + 1) * blk, :] + bias
            m = jnp.max(s, axis=-1, keepdims=True)
            p = jnp.exp(s - m)
            pbuf[slot, h * blk:(h + 1) * blk, :] = p.astype(BF16)
            stats = jnp.where(st_lane == h, m, stats)
            stats = jnp.where(st_lane == n_heads + h, jnp.sum(p, axis=-1, keepdims=True), stats)
        st_ref[pl.ds(rows_of(f), blk), :] = stats

    def values(f, slot):
        o_all = jnp.dot(pbuf[slot], v_ref[pl.ds(window_of(f)[0], 2 * blk), :],
                        preferred_element_type=F32)
        o = o_all[0:blk]
        for h in range(1, n_heads):
            o = jnp.where(head_of_lane == h, o_all[h * blk:(h + 1) * blk], o)
        o_ref[pl.ds(rows_of(f), blk), :] = o.astype(BF16)

    def step(f, slot):
        values(f - 2, slot)
        scores(f, slot)
        softmax(f - 1, 1 - slot)

    scores(0, 0)
    scores(1, 1)
    softmax(0, 0)

    def pair(it, carry):
        step(2 * it + 2, 0)
        step(2 * it + 3, 1)
        return carry

    lax.fori_loop(0, (nblk - 2) // 2, pair, 0)
    softmax(nblk - 1, 1)
    values(nblk - 2, 0)
    values(nblk - 1, 1)


ST_LANES = LANES


def _attention(q, k, v):
    batch, dilation, rows_, wd = q.shape
    seq = dilation * rows_
    n_heads = wd // HEAD_DIM
    qb = rows_ // ATT_BLK
    assert qb >= 2 and qb & (qb - 1) == 0
    flat = lambda a: a.reshape(batch, seq, wd)
    spec = pl.BlockSpec((None, seq, wd), lambda b: (b, 0, 0))
    o, st = pl.pallas_call(
        functools.partial(_attn_body, n_heads=n_heads, qb=qb),
        grid=(batch,),
        in_specs=[spec, spec, spec],
        out_specs=[spec, pl.BlockSpec((None, seq, ST_LANES), lambda b: (b, 0, 0))],
        out_shape=[jax.ShapeDtypeStruct((batch, seq, wd), BF16),
                   jax.ShapeDtypeStruct((batch, seq, ST_LANES), F32)],
        scratch_shapes=[pltpu.VMEM((2, n_heads * ATT_BLK, 2 * ATT_BLK), F32),
                        pltpu.VMEM((2, n_heads * ATT_BLK, 2 * ATT_BLK), BF16)],
        compiler_params=_cparams("parallel"),
        name=f"attn_d{dilation}",
    )(flat(q), flat(k), flat(v))
    return (o.reshape(batch, dilation, rows_, wd), st.reshape(batch, dilation, rows_, ST_LANES))


def _token_order(block, d, bufs):
    slabs, tm, _ = bufs[0].shape
    stage = 0
    while d > 1:
        f = 4 if d % 4 == 0 else d
        nxt, rows, dst = d // f, tm // d, bufs[stage % 2]
        for r1 in range(nxt):
            for a in range(f):
                blk = block(r1 + nxt * a)
                for s in range(slabs):
                    dst[s, pl.ds(r1 * (tm // nxt) + a, rows, stride=f), :] = (
                        blk[:, s * LANES:(s + 1) * LANES])

        def block(r, dst=dst, size=tm // nxt):
            return jnp.concatenate([dst[s, pl.ds(r * size, size), :] for s in range(slabs)], axis=1)

        d, stage = nxt, stage + 1
    return block(0)


def _att_mix(o_refs, s_refs, gn_ref, oscr, sscr, dilations):
    tm, wd = oscr.shape[3], oscr.shape[2] * LANES
    n_heads = wd // HEAD_DIM
    outs, lses = [], []
    for p, d in enumerate(dilations):
        outs.append(_token_order(lambda r, p=p: o_refs[p][r].astype(F32), d,
                                 (oscr.at[p, 0], oscr.at[p, 1])))
        lses.append(_token_order(lambda r, p=p: s_refs[p][r], d, (sscr.at[p, 0], sscr.at[p, 1])))
    m = functools.reduce(jnp.maximum, lses)
    head_of_lane = lax.broadcasted_iota(jnp.int32, (1, wd), 1) // HEAD_DIM

    def widen(x, lane0):
        wide = jnp.zeros((tm, wd), F32)
        for h in range(n_heads):
            wide = jnp.where(head_of_lane == h, x[:, lane0 + h:lane0 + h + 1], wide)
        return wide

    num = jnp.zeros((tm, wd), F32)
    den = jnp.zeros((tm, wd), F32)
    for st, o in zip(lses, outs):
        e = widen(jnp.exp(st - m), 0)
        num = num + e * o
        den = den + e * widen(st, n_heads)
    return _rms(num / den, gn_ref[...]).astype(BF16)


FF_CHUNK = 256


def _out_ffn_body(x_ref, ys_ref, *rest, final_norm, dilations, tiles, per_seq):
    npat = len(dilations)
    conv_in, rest = rest[:10], rest[10:]
    o_refs, s_refs = rest[:npat], rest[npat:2 * npat]
    (gna_ref, wo_ref, gf_ref, wg_ref, wu_ref, wd_ref, gl_ref, o_ref, act, ya, oscr, sscr, ycc, zb3,
     zs) = rest[2 * npat:]
    g = pl.program_id(0)
    first_of_seq = jnp.minimum(g, tiles - 1) % per_seq == 0

    @pl.when(g == 0)
    def _():
        ya[1] = jnp.zeros(ya.shape[1:], BF16)
        ycc[1] = jnp.zeros(ycc.shape[1:], BF16)

    cur, prev = g % 2, (g + 1) % 2
    mixed = jnp.concatenate([ycc[prev], ys_ref[...], ya[prev]], axis=1)
    x1 = x_ref[...] + jnp.dot(mixed, wo_ref[...], preferred_element_type=F32)
    o_ref[...] = x1
    h = _rms(x1, gf_ref[...]).astype(BF16)
    dff = wg_ref.shape[1]
    for c in range(dff // FF_CHUNK):
        sl = slice(c * FF_CHUNK, (c + 1) * FF_CHUNK)
        gate = jnp.dot(h, wg_ref[:, sl], preferred_element_type=F32)
        up = jnp.dot(h, wu_ref[:, sl], preferred_element_type=F32)
        act[:, sl] = (gate * jax.nn.sigmoid(gate) * up).astype(BF16)
    x2 = o_ref[...] + jnp.dot(act[...], wd_ref[...], preferred_element_type=F32)
    if final_norm:
        x2 = _rms(x2, gl_ref[...])
    o_ref[...] = x2

    ya[cur] = _att_mix(o_refs, s_refs, gna_ref, oscr, sscr, dilations)
    _conv_tile(*conv_in, ycc.at[cur], zb3, zs, jnp.where(first_of_seq, 0.0, 1.0))


def _out_ffn(x, conv_args, ys, att_outs, att_stats, gna, wo, gf, wg, wu, wd, gl, final_norm, tm):
    n, d = x.shape
    dff = wg.shape[1]
    batch, _, _, wa = att_outs[0].shape
    seq = n // batch
    dilations = tuple(o.shape[1] for o in att_outs)
    per_seq = seq // tm
    tiles = n // tm
    row = lambda g: (jnp.maximum(g - 1, 0), 0)
    ahead = lambda g: jnp.minimum(g, tiles - 1)
    conv_ops, conv_specs, conv_scratch = _conv_operands(*conv_args, tm, ahead)
    wcc = conv_ops[-1].shape[1]

    def res_spec(dl, width):
        return pl.BlockSpec((None, dl, tm // dl, width),
                            lambda g: (ahead(g) // per_seq, 0, ahead(g) % per_seq, 0))

    return pl.pallas_call(
        functools.partial(_out_ffn_body, final_norm=final_norm, dilations=dilations, tiles=tiles,
                          per_seq=per_seq),
        grid=(tiles + 1,),
        in_specs=[pl.BlockSpec((tm, d), row), pl.BlockSpec((tm, ys.shape[1]), row)] + conv_specs
        + [res_spec(dl, wa) for dl in dilations] + [res_spec(dl, ST_LANES) for dl in dilations]
        + [_resident((1, wa)), _resident(wo.shape), _resident((1, d)), _resident(wg.shape),
           _resident(wu.shape), _resident(wd.shape), _resident((1, d))],
        out_specs=pl.BlockSpec((tm, d), row),
        out_shape=jax.ShapeDtypeStruct((n, d), F32),
        scratch_shapes=[pltpu.VMEM((tm, dff), BF16), pltpu.VMEM((2, tm, wa), BF16),
                        pltpu.VMEM((len(dilations), 2, wa // LANES, tm, LANES), F32),
                        pltpu.VMEM((len(dilations), 2, ST_LANES // LANES, tm, LANES), F32),
                        pltpu.VMEM((2, tm, wcc), BF16)] + conv_scratch,
        compiler_params=_cparams("arbitrary"),
        name="out_ffn",
    )(x, ys, *conv_ops, *att_outs, *att_stats, gna.reshape(1, wa), wo, gf.reshape(1, d), wg, wu,
      wd, gl.reshape(1, d))


def _rope_tables(seq, wd):
    half = HEAD_DIM // 2
    inv = ROPE_THETA ** (-jnp.arange(0, HEAD_DIM, 2, dtype=F32) / HEAD_DIM)
    ang = jnp.arange(seq, dtype=F32)[:, None] * inv[None, :]
    cos, sin = jnp.cos(ang), jnp.sin(ang)
    reps = wd // HEAD_DIM
    cos_t = jnp.tile(jnp.concatenate([cos, cos], axis=1), (1, reps))
    sin_t = jnp.tile(jnp.concatenate([-sin, sin], axis=1), (1, reps))
    scale = HEAD_DIM ** -0.5
    return cos_t * scale, sin_t * scale, cos_t, sin_t


def kernel(x, norm_mix_g, w_in, conv3_w, cfm_dw_w, cfm_dw_b, cfm_ln_g, cfm_ln_b, s5_a_re, s5_a_im, s5_log_dt, s5_b_re, s5_b_im, s5_c_re, s5_c_im, s5_d, s5_glu_w, grp_norm_g, w_out, norm_ffn_g, w_gate, w_up, w_down, final_norm_g):
    batch, seq, d_model = x.shape
    depth = w_in.shape[0]
    wmix = w_in.shape[2] // 9
    n = batch * seq
    assert wmix == 2 * 8 * SSM_CH and all(w // d == ATT_BLK for w, d in DILATED_CFG)
    assert seq % (DILATED_CFG[-1][1] * ATT_BLK) == 0 and seq % 512 == 0
    tm = 512
    group = math.gcd(batch, 8)
    rope_tabs = _rope_tables(seq, wmix)

    dilations = tuple(dil for _, dil in DILATED_CFG)
    npat = len(dilations)

    w_in, s5_glu_w, w_out, w_gate, w_up, w_down = (
        a.astype(BF16) for a in (w_in, s5_glu_w, w_out, w_gate, w_up, w_down))
    lbt, ct, clt, lam8 = jax.vmap(_ssm_discretise)(s5_a_re, s5_a_im, s5_log_dt, s5_b_re, s5_b_im,
                                                   s5_c_re, s5_c_im)
    ssm_mats = (*_ssm_fold(lbt, ct, clt), lam8)

    xf = x.reshape(n, d_model)
    for layer in range(depth):
        pc, pf, u, *qkv = _in_proj(xf, norm_mix_g[layer], w_in[layer], rope_tabs,
                                   batch, seq, 2 * tm, dilations, group)
        gn = grp_norm_g[layer]
        conv_args = (pc, pf, conv3_w[layer], cfm_dw_w[layer], cfm_dw_b[layer], cfm_ln_g[layer],
                     cfm_ln_b[layer], gn[:2 * wmix])
        ys = _ssm_mixer(u, [a[layer] for a in ssm_mats], s5_d[layer], s5_glu_w[layer],
                        gn[2 * wmix:3 * wmix], seq, tm // SSM_CHUNK)
        outs, stats = zip(*[_attention(qkv[p], qkv[npat + p], qkv[2 * npat + p])
                            for p in range(npat)])
        xf = _out_ffn(xf, conv_args, ys, outs, stats, gn[3 * wmix:], w_out[layer],
                      norm_ffn_g[layer],
                      w_gate[layer], w_up[layer], w_down[layer], final_norm_g,
                      layer == depth - 1, tm)
    return xf.reshape(batch, seq, d_model)
```

```python
import functools
import math

import jax
import jax.numpy as jnp
from jax import lax
from jax.experimental import pallas as pl
from jax.experimental.pallas import tpu as pltpu

F32 = jnp.float32
BF16 = jnp.bfloat16

EPS = 1e-6
HEAD_DIM = 64
SHORT_K = 3
CFM_K = 31
SSM_CH = 16
SSM_STATE = 64
SSM_CHUNK = 8
DILATED_CFG = ((128, 1), (512, 4), (2048, 16))
ROPE_THETA = 10000.0
ATT_BLK = 128
NEG_BIG = -1e30
VMEM_LIMIT = 56 * 1024 * 1024


def _cparams(*sem):
    return pltpu.CompilerParams(dimension_semantics=sem, vmem_limit_bytes=VMEM_LIMIT)


def _rms(x, g):
    return x * lax.rsqrt(jnp.mean(x * x, axis=-1, keepdims=True) + EPS) * g


def _resident(shape):
    nd = len(shape)
    return pl.BlockSpec(shape, lambda *_: (0,) * nd, pipeline_mode=pl.Buffered(1))


LANES = 128


def _to_slabs(scr, val):
    for s in range(scr.shape[0]):
        scr[s] = val[:, s * LANES:(s + 1) * LANES]


def _rows_strided(scr, start, count, stride):
    return jnp.concatenate([scr[s, pl.ds(start, count, stride=stride), :]
                            for s in range(scr.shape[0])], axis=1)


def _in_proj_body(x_ref, g_ref, w_ref, cq_ref, sq_ref, ck_ref, sk_ref, pc_ref, pf_ref, u_ref,
                  *rest, wmix, dilations):
    qkv_refs, scr = rest[:-1], rest[-1]
    tm = x_ref.shape[0]
    h = _rms(x_ref[...], g_ref[...]).astype(BF16)

    def proj(lo, width):
        return jnp.dot(h, w_ref[:, lo:lo + width], preferred_element_type=F32)

    _to_slabs(scr.at[3, 0], proj(5 * wmix, wmix))
    for t in range(SSM_CHUNK):
        u_ref[:, t * wmix:(t + 1) * wmix] = _rows_strided(
            scr.at[3, 0], t, tm // SSM_CHUNK, SSM_CHUNK).astype(BF16)

    lane = lax.broadcasted_iota(jnp.int32, (1, wmix), 1)
    first_half = (lane % HEAD_DIM) < (HEAD_DIM // 2)

    def rope(t, c_ref, s_ref):
        partner = jnp.where(first_half, pltpu.roll(t, wmix - HEAD_DIM // 2, 1),
                            pltpu.roll(t, HEAD_DIM // 2, 1))
        return t * c_ref[...] + partner * s_ref[...]

    vals = (rope(proj(6 * wmix, wmix), cq_ref, sq_ref), rope(proj(7 * wmix, wmix), ck_ref, sk_ref),
            proj(8 * wmix, wmix))
    for i, val in enumerate(vals):
        outs = qkv_refs[i * len(dilations):(i + 1) * len(dilations)]
        prev, stage = 1, 0
        for n_d, (d, o_ref) in enumerate(zip(dilations, outs)):
            if d == 1:
                o_ref[0] = val.astype(BF16)
                continue
            src, dst = scr.at[i, stage % 2], scr.at[i, 1 - stage % 2]
            if stage == 0:
                _to_slabs(src, val)
            ratio, rows = d // prev, tm // d
            for r in range(prev):
                for a in range(ratio):
                    part = _rows_strided(src, r * (tm // prev) + a, rows, ratio)
                    o_ref[r + prev * a] = part.astype(BF16)
                    if n_d + 1 < len(dilations):
                        _to_slabs(dst.at[:, pl.ds((r + prev * a) * rows, rows)], part)
            prev, stage = d, stage + 1

    pc_ref[...] = proj(0, 3 * wmix).astype(BF16)
    pf_ref[...] = proj(3 * wmix, 2 * wmix).astype(BF16)


def _in_proj(x, g, w, rope_tabs, batch, seq, tm, dilations, group):
    n, d = x.shape
    nin = w.shape[1]
    wmix = nin // 9
    per_seq = seq // tm
    row = lambda i: (i, 0)
    tab = lambda i: (i % per_seq, 0)
    flat = [(n, 3 * wmix), (n, 2 * wmix)]
    flat_blocks = [(tm, 3 * wmix), (tm, 2 * wmix)]
    uw = SSM_CHUNK * wmix
    u_shape = (batch // group, seq // SSM_CHUNK, group * uw)
    u_spec = pl.BlockSpec((None, tm // SSM_CHUNK, uw),
                          lambda i: (i // per_seq // group, i % per_seq, i // per_seq % group))
    res_shapes = [(batch, dl, seq // dl, wmix) for dl in dilations] * 3
    res_specs = [pl.BlockSpec((None, dl, tm // dl, wmix),
                              lambda i: (i // per_seq, 0, i % per_seq, 0)) for dl in dilations] * 3
    outs = pl.pallas_call(
        functools.partial(_in_proj_body, wmix=wmix, dilations=dilations),
        grid=(n // tm,),
        in_specs=[pl.BlockSpec((tm, d), row), _resident((1, d)), _resident((d, nin))]
        + [pl.BlockSpec((tm, wmix), tab)] * 4,
        out_specs=[pl.BlockSpec(b, row) for b in flat_blocks] + [u_spec] + res_specs,
        out_shape=[jax.ShapeDtypeStruct(s, BF16) for s in flat + [u_shape] + res_shapes],
        scratch_shapes=[pltpu.VMEM((4, 2, wmix // LANES, tm, LANES), F32)],
        compiler_params=_cparams("parallel"),
        name="in_proj",
    )(x, g.reshape(1, d), w, *rope_tabs)
    return outs


CONV_ROWS = 32
CFM_HALO = 32
SHORT_HALO = 16


def _conv_tile(pc_ref, pch_ref, pf_ref, pfh_ref, w3_ref, wdw_ref, bdw_ref, lng_ref, lnb_ref,
               gn_ref, o_ref, zb3, zs, keep):
    t_rows = pc_ref.shape[0]
    wmix = pf_ref.shape[1] // 2

    pc = pc_ref[...].astype(F32)
    ch = pc[:, 2 * wmix:] * pc[:, :wmix]
    hal = pch_ref[...].astype(F32)
    zb3[0:SHORT_HALO, :] = hal[:, 2 * wmix:] * hal[:, :wmix] * keep
    zb3[SHORT_HALO:, :] = ch
    conv = w3_ref[SHORT_K - 1:SHORT_K, :] * ch
    for k in range(SHORT_K - 1):
        off = SHORT_HALO - (SHORT_K - 1) + k
        conv = conv + w3_ref[k:k + 1, :] * zb3[off:off + t_rows, :]
    y_conv = pc[:, wmix:2 * wmix] * conv
    o_ref[:, :wmix] = _rms(y_conv, gn_ref[:, :wmix]).astype(BF16)

    pf = pf_ref[...].astype(F32)
    ph = pfh_ref[...].astype(F32)
    zs[0, 0:CFM_HALO, :] = ph[:, :wmix] * jax.nn.sigmoid(ph[:, wmix:]) * keep
    zs[0, CFM_HALO:, :] = pf[:, :wmix] * jax.nn.sigmoid(pf[:, wmix:])
    span = t_rows + CFM_HALO - 8
    for b in range(1, 8):
        zs[b, 0:span, :] = zs[0, b:b + span, :]

    base = CFM_HALO - (CFM_K - 1)

    for c in range(t_rows // CONV_ROWS):
        r0 = c * CONV_ROWS
        accs = [None] * 4
        for b in range(8):
            ks = [k for k in range(CFM_K) if (base + k) % 8 == b]
            hi = max((base + k) // 8 for k in ks)
            win = zs[b, pl.ds(r0, CONV_ROWS + 8 * hi), :].reshape(CONV_ROWS // 8 + hi, 8, wmix)
            for k in ks:
                a = (base + k) // 8
                term = wdw_ref[k] * win[a:a + CONV_ROWS // 8]
                accs[k % 4] = term if accs[k % 4] is None else accs[k % 4] + term
        acc = (accs[0] + accs[1]) + (accs[2] + accs[3])
        zb3[pl.ds(r0, CONV_ROWS), :] = acc.reshape(CONV_ROWS, wmix) + bdw_ref[...]

    acc = zb3[0:t_rows, :]
    mu = jnp.mean(acc, axis=-1, keepdims=True)
    xc = acc - mu
    var = jnp.mean(xc * xc, axis=-1, keepdims=True)
    y = xc * lax.rsqrt(var + EPS) * lng_ref[...] + lnb_ref[...]
    y = y * jax.nn.sigmoid(y)
    o_ref[:, wmix:] = _rms(y, gn_ref[:, wmix:]).astype(BF16)


def _conv_operands(pc, pf, w3, wdw, bdw, lng, lnb, gn, tt, tile_of_step):
    wmix = pf.shape[1] // 2
    row = lambda g: (tile_of_step(g), 0)

    def halo(rows_):
        return lambda g: (jnp.maximum(tile_of_step(g) * (tt // rows_) - 1, 0), 0)

    vec = lambda a: a.reshape(1, -1)
    operands = (pc, pc, pf, pf, w3, jnp.broadcast_to(wdw[:, None, :], (CFM_K, 8, wmix)), vec(bdw),
                vec(lng), vec(lnb), vec(gn))
    specs = [pl.BlockSpec((tt, 3 * wmix), row),
             pl.BlockSpec((SHORT_HALO, 3 * wmix), halo(SHORT_HALO)),
             pl.BlockSpec((tt, 2 * wmix), row),
             pl.BlockSpec((CFM_HALO, 2 * wmix), halo(CFM_HALO)),
             _resident((SHORT_K, wmix)), _resident((CFM_K, 8, wmix)),
             _resident((1, wmix)), _resident((1, wmix)), _resident((1, wmix)),
             _resident((1, 2 * wmix))]
    scratch = [pltpu.VMEM((tt + SHORT_HALO, wmix), F32), pltpu.VMEM((8, tt + CFM_HALO, wmix), F32)]
    return operands, specs, scratch


def _cmul(ar, ai, br, bi):
    return ar * br - ai * bi, ar * bi + ai * br


def _ssm_discretise(a_re, a_im, log_dt, b_re, b_im, c_re, c_im):
    ng, ns = a_re.shape
    nh = b_re.shape[-1]
    tc = SSM_CHUNK
    dt = jnp.exp(log_dt)[:, None]
    mag = jnp.exp(a_re * dt)
    lr, li = mag * jnp.cos(a_im * dt), mag * jnp.sin(a_im * dt)
    den = a_re * a_re + a_im * a_im
    nr, ni = lr - 1.0, li
    fr = (nr * a_re + ni * a_im) / den
    fi = (ni * a_re - nr * a_im) / den
    bbr = fr[..., None] * b_re - fi[..., None] * b_im
    bbi = fr[..., None] * b_im + fi[..., None] * b_re

    pr, pi = [jnp.ones_like(lr)], [jnp.zeros_like(li)]
    for _ in range(tc):
        r, i = _cmul(pr[-1], pi[-1], lr, li)
        pr.append(r)
        pi.append(i)
    pwr, pwi = jnp.stack(pr), jnp.stack(pi)

    lbr, lbi = _cmul(pwr[:tc, :, :, None], pwi[:tc, :, :, None], bbr[None], bbi[None])
    lbt = jnp.stack([lbr, lbi]).transpose(0, 1, 2, 4, 3).reshape(2, tc, ng * nh, ns)
    ct = jnp.stack([c_re, c_im]).transpose(0, 1, 3, 2).reshape(2, ng * ns, nh)
    clr, cli = _cmul(c_re[None], c_im[None], pwr[1:, :, None, :], pwi[1:, :, None, :])
    clt = jnp.stack([clr, cli]).transpose(0, 1, 2, 4, 3).reshape(2, tc, ng * ns, nh)
    lam8 = jnp.concatenate([pwr[tc].reshape(2, 1, -1), pwi[tc].reshape(2, 1, -1)], axis=2)
    return lbt, ct, clt, lam8


def _split_bf16(x, terms):
    parts = []
    for _ in range(terms):
        part = x.astype(BF16)
        parts.append(part)
        x = x - part.astype(F32)
    return parts


def _ssm_fold_body(lbt_ref, ct_ref, clt_ref, m_ref, w_ref, g_ref):
    tc = lbt_ref.shape[1]
    rows, ns = lbt_ref.shape[2], lbt_ref.shape[3]
    nh = ct_ref.shape[2]
    gh = rows // nh
    srows = gh * ns

    def grid2(shape):
        return (lax.broadcasted_iota(jnp.int32, shape, 0), lax.broadcasted_iota(jnp.int32, shape, 1))

    r, c = grid2((ns, srows))
    spread_p = jnp.where(c % ns == r, 1.0, 0.0).astype(BF16)
    r, c = grid2((nh, rows))
    spread_h = jnp.where(c % nh == r, 1.0, 0.0).astype(BF16)
    r, c = grid2((rows, srows))
    mask_w = r // nh == c // ns
    r, c = grid2((rows, rows))
    mask_m = r // nh == c // nh
    r, c = grid2((srows, rows))
    mask_g = r // ns == c // nh

    def dot(a, b):
        return jnp.dot(a, b, preferred_element_type=F32)

    lb = [[jnp.where(mask_w, sum(dot(t, spread_p) for t in _split_bf16(lbt_ref[ri, d], 3)), 0.0)
           for d in range(tc)] for ri in range(2)]
    for tau in range(tc):
        w_ref[tau * rows:(tau + 1) * rows, :] = jnp.concatenate(
            [lb[0][tc - 1 - tau], lb[1][tc - 1 - tau]], axis=1).astype(BF16)

    c_parts = [_split_bf16(ct_ref[ri], 2) for ri in range(2)]

    def dot_f32(x, y_parts):
        xh, xl = _split_bf16(x, 2)
        return dot(xh, y_parts[0]) + dot(xh, y_parts[1]) + dot(xl, y_parts[0])

    zero = jnp.zeros((rows, rows), BF16)
    blocks = []
    for d in range(tc):
        kd = dot_f32(lb[0][d], c_parts[0]) - dot_f32(lb[1][d], c_parts[1])
        blocks.append(jnp.where(mask_m, dot(kd.astype(BF16), spread_h), 0.0).astype(BF16))
    for tau in range(tc):
        for t in range(tc):
            m_ref[tau * rows:(tau + 1) * rows, t * rows:(t + 1) * rows] = (
                blocks[t - tau] if t >= tau else zero)

    for t in range(tc):
        for ri in range(2):
            blk = jnp.where(mask_g, dot(clt_ref[ri, t].astype(BF16), spread_h), 0.0)
            g_ref[ri * srows:(ri + 1) * srows, t * rows:(t + 1) * rows] = (
                (-blk if ri else blk).astype(BF16))


def _ssm_fold(lbt, ct, clt):
    depth, _, tc, ghn, ns = lbt.shape
    nh = ct.shape[3]
    rows = ghn // 2
    srows = ct.shape[2] // 2
    out = jax.ShapeDtypeStruct((depth, 2, tc * rows, tc * rows), BF16)
    out_spec = pl.BlockSpec((None, None, tc * rows, tc * rows), lambda l, j: (l, j, 0, 0))
    assert 2 * srows == tc * rows
    return pl.pallas_call(
        _ssm_fold_body,
        grid=(depth, 2),
        in_specs=[pl.BlockSpec((None, 2, tc, rows, ns), lambda l, j: (l, 0, 0, j, 0)),
                  pl.BlockSpec((None, 2, srows, nh), lambda l, j: (l, 0, j, 0)),
                  pl.BlockSpec((None, 2, tc, srows, nh), lambda l, j: (l, 0, 0, j, 0))],
        out_specs=[out_spec] * 3,
        out_shape=[out] * 3,
        compiler_params=_cparams("parallel", "parallel"),
        name="ssm_fold",
    )(lbt, ct, clt)


def _ssm_body(u_ref, m_ref, w_ref, g_ref, a_ref, d_ref, wglu_ref, gn_ref, o_ref, state, zbuf, sbuf,
              zrow, ysc, uscr, tok, *, half, group):
    steps = u_ref.shape[0]
    width = u_ref.shape[1] // group
    ch = 2 * half
    nst = a_ref.shape[2] // 2
    assert half == LANES

    for b in range(group):
        part = u_ref[:, b * width:(b + 1) * width].astype(F32)
        for s in range(width // LANES):
            uscr[s, pl.ds(b, steps, stride=group), :] = part[:, s * LANES:(s + 1) * LANES]

    @pl.when(pl.program_id(1) == 0)
    def _():
        state[...] = jnp.zeros(state.shape, F32)

    lanes = [[slice(t * ch + j * half, t * ch + (j + 1) * half) for t in range(SSM_CHUNK)]
             for j in range(2)]
    lhs = [jnp.concatenate([uscr[s.start // LANES] for s in lanes[j]], axis=1).astype(BF16)
           for j in range(2)]
    for j in range(2):
        zbuf[j] = jnp.dot(lhs[j], w_ref[j], preferred_element_type=F32)

    lam = [(jnp.broadcast_to(a_ref[j, :, :nst], (group, nst)),
            jnp.broadcast_to(a_ref[j, :, nst:], (group, nst))) for j in range(2)]

    def chunk(r, carry):
        r0 = pl.multiple_of(r * group, group)
        out = []
        for j in range(2):
            sr, si = carry[2 * j], carry[2 * j + 1]
            sbuf[j, pl.ds(r0, group), :] = jnp.concatenate([sr, si], axis=1)
            z = zbuf[j, pl.ds(r0, group), :]
            ar, ai = lam[j]
            out += [ar * sr - ai * si + z[:, :nst], ar * si + ai * sr + z[:, nst:]]
        return tuple(out)

    final = lax.fori_loop(0, steps, chunk, tuple(state[i] for i in range(4)), unroll=True)
    for i in range(4):
        state[i] = final[i]

    for j in range(2):
        y = (jnp.dot(lhs[j], m_ref[j], preferred_element_type=F32)
             + jnp.dot(sbuf[j].astype(BF16), g_ref[j], preferred_element_type=F32))
        for t, s in enumerate(lanes[j]):
            yy = y[:, t * half:(t + 1) * half] + d_ref[:, s] * uscr[s.start // LANES]
            zrow[:, s] = jax.nn.gelu(yy).astype(BF16)

    for t in range(SSM_CHUNK):
        gl = jnp.dot(zrow[:, t * ch:(t + 1) * ch], wglu_ref[...], preferred_element_type=F32)
        _to_slabs(ysc, _rms(gl[:, :ch] * jax.nn.sigmoid(gl[:, ch:]), gn_ref[...]))
        for b in range(group):
            for s in range(ysc.shape[0]):
                tok[b, s, pl.ds(t, steps, stride=SSM_CHUNK), :] = (
                    ysc[s, pl.ds(b, steps, stride=group), :])
    for b in range(group):
        o_ref[b] = jnp.concatenate([tok[b, s] for s in range(tok.shape[1])], axis=1).astype(BF16)


def _ssm_mixer(u_rows, mats, d_skip, wglu, gn, seq, tile_rows):
    n_groups = u_rows.shape[0]
    width = SSM_CHUNK * wglu.shape[0]
    ch = width // SSM_CHUNK
    m, w, g, a = mats
    group = u_rows.shape[2] // width
    rows_ = tile_rows * group
    d_row = jnp.tile(d_skip, SSM_CHUNK).reshape(1, width)
    nst = a.shape[2] // 2
    out = pl.pallas_call(
        functools.partial(_ssm_body, half=ch // 2, group=group),
        grid=(n_groups, seq // SSM_CHUNK // tile_rows),
        in_specs=[pl.BlockSpec((None, tile_rows, group * width), lambda b, t: (b, t, 0)),
                  _resident(m.shape), _resident(w.shape), _resident(g.shape),
                  _resident(a.shape), _resident((1, width)), _resident(wglu.shape),
                  _resident((1, ch))],
        out_specs=pl.BlockSpec((group, tile_rows * SSM_CHUNK, ch), lambda b, t: (b, t, 0)),
        out_shape=jax.ShapeDtypeStruct((n_groups * group, seq, ch), BF16),
        scratch_shapes=[pltpu.VMEM((4, group, nst), F32),
                        pltpu.VMEM((2, rows_, 2 * nst), F32), pltpu.VMEM((2, rows_, 2 * nst), F32),
                        pltpu.VMEM((rows_, width), BF16),
                        pltpu.VMEM((ch // LANES, rows_, LANES), F32),
                        pltpu.VMEM((width // LANES, rows_, LANES), F32),
                        pltpu.VMEM((group, ch // LANES, tile_rows * SSM_CHUNK, LANES), F32)],
        compiler_params=_cparams("parallel", "arbitrary"),
        name="ssm_mixer",
    )(u_rows, m, w, g, a, d_row, wglu, gn.reshape(1, ch))
    return out.reshape(n_groups * group * seq, ch)


def _attn_body(q_ref, k_ref, v_ref, o_ref, st_ref, sbuf, pbuf, *, n_heads, qb):
    blk = ATT_BLK
    wd = q_ref.shape[1]
    nblk = q_ref.shape[0] // blk

    qi = lax.broadcasted_iota(jnp.int32, (blk, 2 * blk), 0)
    kj = lax.broadcasted_iota(jnp.int32, (blk, 2 * blk), 1)
    rel = blk + qi - kj
    bias_any = jnp.where((rel >= 0) & (rel <= blk), 0.0, NEG_BIG)
    bias_first = jnp.where(kj <= qi, 0.0, NEG_BIG)
    head_of_lane = lax.broadcasted_iota(jnp.int32, (1, wd), 1) // HEAD_DIM
    st_lane = lax.broadcasted_iota(jnp.int32, (1, st_ref.shape[1]), 1)

    def rows_of(f):
        return f * blk if isinstance(f, int) else pl.multiple_of(f * blk, blk)

    def window_of(f):
        first = (f % qb) == 0
        if isinstance(f, int):
            return (f if first else f - 1) * blk, first
        return pl.multiple_of((f - jnp.where(first, 0, 1)) * blk, blk), first

    def scores(f, slot):
        q = q_ref[pl.ds(rows_of(f), blk), :]
        qs = jnp.concatenate([jnp.where(head_of_lane == h, q, jnp.zeros_like(q))
                              for h in range(n_heads)], axis=0)
        sbuf[slot] = lax.dot_general(qs, k_ref[pl.ds(window_of(f)[0], 2 * blk), :],
                                     (((1,), (1,)), ((), ())), preferred_element_type=F32)

    def softmax(f, slot):
        first = window_of(f)[1]
        bias = (bias_first if first else bias_any) if isinstance(f, int) else jnp.where(
            first, bias_first, bias_any)
        stats = jnp.zeros((blk, st_ref.shape[1]), F32)
        for h in range(n_heads):
            s = sbuf[slot, h * blk:(h + 1) * blk, :] + bias
            m = jnp.max(s, axis=-1, keepdims=True)
            p = jnp.exp(s - m)
            pbuf[slot, h * blk:(h + 1) * blk, :] = p.astype(BF16)
            stats = jnp.where(st_lane == h, m, stats)
            stats = jnp.where(st_lane == n_heads + h, jnp.sum(p, axis=-1, keepdims=True), stats)
        st_ref[pl.ds(rows_of(f), blk), :] = stats

    def values(f, slot):
        o_all = jnp.dot(pbuf[slot], v_ref[pl.ds(window_of(f)[0], 2 * blk), :],
                        preferred_element_type=F32)
        o = o_all[0:blk]
        for h in range(1, n_heads):
            o = jnp.where(head_of_lane == h, o_all[h * blk:(h + 1) * blk], o)
        o_ref[pl.ds(rows_of(f), blk), :] = o.astype(BF16)

    def step(f, slot):
        values(f - 2, slot)
        scores(f, slot)
        softmax(f - 1, 1 - slot)

    scores(0, 0)
    scores(1, 1)
    softmax(0, 0)

    def pair(it, carry):
        step(2 * it + 2, 0)
        step(2 * it + 3, 1)
        return carry

    lax.fori_loop(0, (nblk - 2) // 2, pair, 0)
    softmax(nblk - 1, 1)
    values(nblk - 2, 0)
    values(nblk - 1, 1)


ST_LANES = LANES


def _attention(q, k, v):
    batch, dilation, rows_, wd = q.shape
    seq = dilation * rows_
    n_heads = wd // HEAD_DIM
    qb = rows_ // ATT_BLK
    assert qb >= 2 and qb & (qb - 1) == 0
    flat = lambda a: a.reshape(batch, seq, wd)
    spec = pl.BlockSpec((None, seq, wd), lambda b: (b, 0, 0))
    o, st = pl.pallas_call(
        functools.partial(_attn_body, n_heads=n_heads, qb=qb),
        grid=(batch,),
        in_specs=[spec, spec, spec],
        out_specs=[spec, pl.BlockSpec((None, seq, ST_LANES), lambda b: (b, 0, 0))],
        out_shape=[jax.ShapeDtypeStruct((batch, seq, wd), BF16),
                   jax.ShapeDtypeStruct((batch, seq, ST_LANES), F32)],
        scratch_shapes=[pltpu.VMEM((2, n_heads * ATT_BLK, 2 * ATT_BLK), F32),
                        pltpu.VMEM((2, n_heads * ATT_BLK, 2 * ATT_BLK), BF16)],
        compiler_params=_cparams("parallel"),
        name=f"attn_d{dilation}",
    )(flat(q), flat(k), flat(v))
    return (o.reshape(batch, dilation, rows_, wd), st.reshape(batch, dilation, rows_, ST_LANES))


def _token_order(block, d, bufs):
    slabs, tm, _ = bufs[0].shape
    stage = 0
    while d > 1:
        f = 4 if d % 4 == 0 else d
        nxt, rows, dst = d // f, tm // d, bufs[stage % 2]
        for r1 in range(nxt):
            for a in range(f):
                blk = block(r1 + nxt * a)
                for s in range(slabs):
                    dst[s, pl.ds(r1 * (tm // nxt) + a, rows, stride=f), :] = (
                        blk[:, s * LANES:(s + 1) * LANES])

        def block(r, dst=dst, size=tm // nxt):
            return jnp.concatenate([dst[s, pl.ds(r * size, size), :] for s in range(slabs)], axis=1)

        d, stage = nxt, stage + 1
    return block(0)


def _att_mix(o_refs, s_refs, gn_ref, oscr, sscr, dilations):
    tm, wd = oscr.shape[3], oscr.shape[2] * LANES
    n_heads = wd // HEAD_DIM
    outs, lses = [], []
    for p, d in enumerate(dilations):
        outs.append(_token_order(lambda r, p=p: o_refs[p][r].astype(F32), d,
                                 (oscr.at[p, 0], oscr.at[p, 1])))
        lses.append(_token_order(lambda r, p=p: s_refs[p][r], d, (sscr.at[p, 0], sscr.at[p, 1])))
    m = functools.reduce(jnp.maximum, lses)
    head_of_lane = lax.broadcasted_iota(jnp.int32, (1, wd), 1) // HEAD_DIM

    def widen(x, lane0):
        wide = jnp.zeros((tm, wd), F32)
        for h in range(n_heads):
            wide = jnp.where(head_of_lane == h, x[:, lane0 + h:lane0 + h + 1], wide)
        return wide

    num = jnp.zeros((tm, wd), F32)
    den = jnp.zeros((tm, wd), F32)
    for st, o in zip(lses, outs):
        e = widen(jnp.exp(st - m), 0)
        num = num + e * o
        den = den + e * widen(st, n_heads)
    return _rms(num / den, gn_ref[...]).astype(BF16)


FF_CHUNK = 256


def _out_ffn_body(x_ref, ys_ref, *rest, final_norm, dilations, tiles, per_seq):
    npat = len(dilations)
    conv_in, rest = rest[:10], rest[10:]
    o_refs, s_refs = rest[:npat], rest[npat:2 * npat]
    (gna_ref, wo_ref, gf_ref, wg_ref, wu_ref, wd_ref, gl_ref, o_ref, act, ya, oscr, sscr, ycc, zb3,
     zs) = rest[2 * npat:]
    g = pl.program_id(0)
    first_of_seq = jnp.minimum(g, tiles - 1) % per_seq == 0

    @pl.when(g == 0)
    def _():
        ya[1] = jnp.zeros(ya.shape[1:], BF16)
        ycc[1] = jnp.zeros(ycc.shape[1:], BF16)

    cur, prev = g % 2, (g + 1) % 2
    mixed = jnp.concatenate([ycc[prev], ys_ref[...], ya[prev]], axis=1)
    x1 = x_ref[...] + jnp.dot(mixed, wo_ref[...], preferred_element_type=F32)
    o_ref[...] = x1
    h = _rms(x1, gf_ref[...]).astype(BF16)
    dff = wg_ref.shape[1]
    for c in range(dff // FF_CHUNK):
        sl = slice(c * FF_CHUNK, (c + 1) * FF_CHUNK)
        gate = jnp.dot(h, wg_ref[:, sl], preferred_element_type=F32)
        up = jnp.dot(h, wu_ref[:, sl], preferred_element_type=F32)
        act[:, sl] = (gate * jax.nn.sigmoid(gate) * up).astype(BF16)
    x2 = o_ref[...] + jnp.dot(act[...], wd_ref[...], preferred_element_type=F32)
    if final_norm:
        x2 = _rms(x2, gl_ref[...])
    o_ref[...] = x2

    ya[cur] = _att_mix(o_refs, s_refs, gna_ref, oscr, sscr, dilations)
    _conv_tile(*conv_in, ycc.at[cur], zb3, zs, jnp.where(first_of_seq, 0.0, 1.0))


def _out_ffn(x, conv_args, ys, att_outs, att_stats, gna, wo, gf, wg, wu, wd, gl, final_norm, tm):
    n, d = x.shape
    dff = wg.shape[1]
    batch, _, _, wa = att_outs[0].shape
    seq = n // batch
    dilations = tuple(o.shape[1] for o in att_outs)
    per_seq = seq // tm
    tiles = n // tm
    row = lambda g: (jnp.maximum(g - 1, 0), 0)
    ahead = lambda g: jnp.minimum(g, tiles - 1)
    conv_ops, conv_specs, conv_scratch = _conv_operands(*conv_args, tm, ahead)
    wcc = conv_ops[-1].shape[1]

    def res_spec(dl, width):
        return pl.BlockSpec((None, dl, tm // dl, width),
                            lambda g: (ahead(g) // per_seq, 0, ahead(g) % per_seq, 0))

    return pl.pallas_call(
        functools.partial(_out_ffn_body, final_norm=final_norm, dilations=dilations, tiles=tiles,
                          per_seq=per_seq),
        grid=(tiles + 1,),
        in_specs=[pl.BlockSpec((tm, d), row), pl.BlockSpec((tm, ys.shape[1]), row)] + conv_specs
        + [res_spec(dl, wa) for dl in dilations] + [res_spec(dl, ST_LANES) for dl in dilations]
        + [_resident((1, wa)), _resident(wo.shape), _resident((1, d)), _resident(wg.shape),
           _resident(wu.shape), _resident(wd.shape), _resident((1, d))],
        out_specs=pl.BlockSpec((tm, d), row),
        out_shape=jax.ShapeDtypeStruct((n, d), F32),
        scratch_shapes=[pltpu.VMEM((tm, dff), BF16), pltpu.VMEM((2, tm, wa), BF16),
                        pltpu.VMEM((len(dilations), 2, wa // LANES, tm, LANES), F32),
                        pltpu.VMEM((len(dilations), 2, ST_LANES // LANES, tm, LANES), F32),
                        pltpu.VMEM((2, tm, wcc), BF16)] + conv_scratch,
        compiler_params=_cparams("arbitrary"),
        name="out_ffn",
    )(x, ys, *conv_ops, *att_outs, *att_stats, gna.reshape(1, wa), wo, gf.reshape(1, d), wg, wu,
      wd, gl.reshape(1, d))


def _rope_tables(seq, wd):
    half = HEAD_DIM // 2
    inv = ROPE_THETA ** (-jnp.arange(0, HEAD_DIM, 2, dtype=F32) / HEAD_DIM)
    ang = jnp.arange(seq, dtype=F32)[:, None] * inv[None, :]
    cos, sin = jnp.cos(ang), jnp.sin(ang)
    reps = wd // HEAD_DIM
    cos_t = jnp.tile(jnp.concatenate([cos, cos], axis=1), (1, reps))
    sin_t = jnp.tile(jnp.concatenate([-sin, sin], axis=1), (1, reps))
    scale = HEAD_DIM ** -0.5
    return cos_t * scale, sin_t * scale, cos_t, sin_t


def kernel(x, norm_mix_g, w_in, conv3_w, cfm_dw_w, cfm_dw_b, cfm_ln_g, cfm_ln_b, s5_a_re, s5_a_im, s5_log_dt, s5_b_re, s5_b_im, s5_c_re, s5_c_im, s5_d, s5_glu_w, grp_norm_g, w_out, norm_ffn_g, w_gate, w_up, w_down, final_norm_g):
    batch, seq, d_model = x.shape
    depth = w_in.shape[0]
    wmix = w_in.shape[2] // 9
    n = batch * seq
    assert wmix == 2 * 8 * SSM_CH and all(w // d == ATT_BLK for w, d in DILATED_CFG)
    assert seq % (DILATED_CFG[-1][1] * ATT_BLK) == 0 and seq % 512 == 0
    tm = 512
    group = math.gcd(batch, 8)
    rope_tabs = _rope_tables(seq, wmix)

    dilations = tuple(dil for _, dil in DILATED_CFG)
    npat = len(dilations)

    w_in, s5_glu_w, w_out, w_gate, w_up, w_down = (
        a.astype(BF16) for a in (w_in, s5_glu_w, w_out, w_gate, w_up, w_down))
    lbt, ct, clt, lam8 = jax.vmap(_ssm_discretise)(s5_a_re, s5_a_im, s5_log_dt, s5_b_re, s5_b_im,
                                                   s5_c_re, s5_c_im)
    ssm_mats = (*_ssm_fold(lbt, ct, clt), lam8)

    xf = x.reshape(n, d_model)
    for layer in range(depth):
        pc, pf, u, *qkv = _in_proj(xf, norm_mix_g[layer], w_in[layer], rope_tabs,
                                   batch, seq, 2 * tm, dilations, group)
        gn = grp_norm_g[layer]
        conv_args = (pc, pf, conv3_w[layer], cfm_dw_w[layer], cfm_dw_b[layer], cfm_ln_g[layer],
                     cfm_ln_b[layer], gn[:2 * wmix])
        ys = _ssm_mixer(u, [a[layer] for a in ssm_mats], s5_d[layer], s5_glu_w[layer],
                        gn[2 * wmix:3 * wmix], seq, tm // SSM_CHUNK)
        outs, stats = zip(*[_attention(qkv[p], qkv[npat + p], qkv[2 * npat + p])
                            for p in range(npat)])
        xf = _out_ffn(xf, conv_args, ys, outs, stats, gn[3 * wmix:], w_out[layer],
                      norm_ffn_g[layer],
                      w_gate[layer], w_up[layer], w_down[layer], final_norm_g,
                      layer == depth - 1, tm)
    return xf.reshape(batch, seq, d_model)
```

```python
import functools
import math

import jax
import jax.numpy as jnp
from jax import lax
from jax.experimental import pallas as pl
from jax.experimental.pallas import tpu as pltpu

F32 = jnp.float32
BF16 = jnp.bfloat16

EPS = 1e-6
HEAD_DIM = 64
SHORT_K = 3
CFM_K = 31
SSM_CH = 16
SSM_STATE = 64
SSM_CHUNK = 8
DILATED_CFG = ((128, 1), (512, 4), (2048, 16))
ROPE_THETA = 10000.0
ATT_BLK = 128
NEG_BIG = -1e30
VMEM_LIMIT = 56 * 1024 * 1024


def _cparams(*sem):
    return pltpu.CompilerParams(dimension_semantics=sem, vmem_limit_bytes=VMEM_LIMIT)


def _rms(x, g):
    return x * lax.rsqrt(jnp.mean(x * x, axis=-1, keepdims=True) + EPS) * g


def _resident(shape):
    nd = len(shape)
    return pl.BlockSpec(shape, lambda *_: (0,) * nd, pipeline_mode=pl.Buffered(1))


LANES = 128


def _to_slabs(scr, val):
    for s in range(scr.shape[0]):
        scr[s] = val[:, s * LANES:(s + 1) * LANES]


def _rows_strided(scr, start, count, stride):
    return jnp.concatenate([scr[s, pl.ds(start, count, stride=stride), :]
                            for s in range(scr.shape[0])], axis=1)


def _in_proj_body(x_ref, g_ref, w_ref, cq_ref, sq_ref, ck_ref, sk_ref, pc_ref, pf_ref, u_ref,
                  *rest, wmix, dilations):
    qkv_refs, scr = rest[:-1], rest[-1]
    tm = x_ref.shape[0]
    h = _rms(x_ref[...], g_ref[...]).astype(BF16)

    def proj(lo, width):
        return jnp.dot(h, w_ref[:, lo:lo + width], preferred_element_type=F32)

    _to_slabs(scr.at[3, 0], proj(5 * wmix, wmix))
    for t in range(SSM_CHUNK):
        u_ref[:, t * wmix:(t + 1) * wmix] = _rows_strided(
            scr.at[3, 0], t, tm // SSM_CHUNK, SSM_CHUNK).astype(BF16)

    lane = lax.broadcasted_iota(jnp.int32, (1, wmix), 1)
    first_half = (lane % HEAD_DIM) < (HEAD_DIM // 2)

    def rope(t, c_ref, s_ref):
        partner = jnp.where(first_half, pltpu.roll(t, wmix - HEAD_DIM // 2, 1),
                            pltpu.roll(t, HEAD_DIM // 2, 1))
        return t * c_ref[...] + partner * s_ref[...]

    vals = (rope(proj(6 * wmix, wmix), cq_ref, sq_ref), rope(proj(7 * wmix, wmix), ck_ref, sk_ref),
            proj(8 * wmix, wmix))
    for i, val in enumerate(vals):
        outs = qkv_refs[i * len(dilations):(i + 1) * len(dilations)]
        prev, stage = 1, 0
        for n_d, (d, o_ref) in enumerate(zip(dilations, outs)):
            if d == 1:
                o_ref[0] = val.astype(BF16)
                continue
            src, dst = scr.at[i, stage % 2], scr.at[i, 1 - stage % 2]
            if stage == 0:
                _to_slabs(src, val)
            ratio, rows = d // prev, tm // d
            for r in range(prev):
                for a in range(ratio):
                    part = _rows_strided(src, r * (tm // prev) + a, rows, ratio)
                    o_ref[r + prev * a] = part.astype(BF16)
                    if n_d + 1 < len(dilations):
                        _to_slabs(dst.at[:, pl.ds((r + prev * a) * rows, rows)], part)
            prev, stage = d, stage + 1

    pc_ref[...] = proj(0, 3 * wmix).astype(BF16)
    pf_ref[...] = proj(3 * wmix, 2 * wmix).astype(BF16)


def _in_proj(x, g, w, rope_tabs, batch, seq, tm, dilations, group):
    n, d = x.shape
    nin = w.shape[1]
    wmix = nin // 9
    per_seq = seq // tm
    row = lambda i: (i, 0)
    tab = lambda i: (i % per_seq, 0)
    flat = [(n, 3 * wmix), (n, 2 * wmix)]
    flat_blocks = [(tm, 3 * wmix), (tm, 2 * wmix)]
    uw = SSM_CHUNK * wmix
    u_shape = (batch // group, seq // SSM_CHUNK, group * uw)
    u_spec = pl.BlockSpec((None, tm // SSM_CHUNK, uw),
                          lambda i: (i // per_seq // group, i % per_seq, i // per_seq % group))
    res_shapes = [(batch, dl, seq // dl, wmix) for dl in dilations] * 3
    res_specs = [pl.BlockSpec((None, dl, tm // dl, wmix),
                              lambda i: (i // per_seq, 0, i % per_seq, 0)) for dl in dilations] * 3
    outs = pl.pallas_call(
        functools.partial(_in_proj_body, wmix=wmix, dilations=dilations),
        grid=(n // tm,),
        in_specs=[pl.BlockSpec((tm, d), row), _resident((1, d)), _resident((d, nin))]
        + [pl.BlockSpec((tm, wmix), tab)] * 4,
        out_specs=[pl.BlockSpec(b, row) for b in flat_blocks] + [u_spec] + res_specs,
        out_shape=[jax.ShapeDtypeStruct(s, BF16) for s in flat + [u_shape] + res_shapes],
        scratch_shapes=[pltpu.VMEM((4, 2, wmix // LANES, tm, LANES), F32)],
        compiler_params=_cparams("parallel"),
        name="in_proj",
    )(x, g.reshape(1, d), w, *rope_tabs)
    return outs


CONV_ROWS = 32
CFM_HALO = 32
SHORT_HALO = 16


def _conv_tile(pc_ref, pch_ref, pf_ref, pfh_ref, w3_ref, wdw_ref, bdw_ref, lng_ref, lnb_ref,
               gn_ref, o_ref, zb3, zs, keep):
    t_rows = pc_ref.shape[0]
    wmix = pf_ref.shape[1] // 2

    pc = pc_ref[...].astype(F32)
    ch = pc[:, 2 * wmix:] * pc[:, :wmix]
    hal = pch_ref[...].astype(F32)
    zb3[0:SHORT_HALO, :] = hal[:, 2 * wmix:] * hal[:, :wmix] * keep
    zb3[SHORT_HALO:, :] = ch
    conv = w3_ref[SHORT_K - 1:SHORT_K, :] * ch
    for k in range(SHORT_K - 1):
        off = SHORT_HALO - (SHORT_K - 1) + k
        conv = conv + w3_ref[k:k + 1, :] * zb3[off:off + t_rows, :]
    y_conv = pc[:, wmix:2 * wmix] * conv
    o_ref[:, :wmix] = _rms(y_conv, gn_ref[:, :wmix]).astype(BF16)

    pf = pf_ref[...].astype(F32)
    ph = pfh_ref[...].astype(F32)
    zs[0, 0:CFM_HALO, :] = ph[:, :wmix] * jax.nn.sigmoid(ph[:, wmix:]) * keep
    zs[0, CFM_HALO:, :] = pf[:, :wmix] * jax.nn.sigmoid(pf[:, wmix:])
    span = t_rows + CFM_HALO - 8
    for b in range(1, 8):
        zs[b, 0:span, :] = zs[0, b:b + span, :]

    base = CFM_HALO - (CFM_K - 1)

    for c in range(t_rows // CONV_ROWS):
        r0 = c * CONV_ROWS
        accs = [None] * 4
        for b in range(8):
            ks = [k for k in range(CFM_K) if (base + k) % 8 == b]
            hi = max((base + k) // 8 for k in ks)
            win = zs[b, pl.ds(r0, CONV_ROWS + 8 * hi), :].reshape(CONV_ROWS // 8 + hi, 8, wmix)
            for k in ks:
                a = (base + k) // 8
                term = wdw_ref[k] * win[a:a + CONV_ROWS // 8]
                accs[k % 4] = term if accs[k % 4] is None else accs[k % 4] + term
        acc = (accs[0] + accs[1]) + (accs[2] + accs[3])
        zb3[pl.ds(r0, CONV_ROWS), :] = acc.reshape(CONV_ROWS, wmix) + bdw_ref[...]

    acc = zb3[0:t_rows, :]
    mu = jnp.mean(acc, axis=-1, keepdims=True)
    xc = acc - mu
    var = jnp.mean(xc * xc, axis=-1, keepdims=True)
    y = xc * lax.rsqrt(var + EPS) * lng_ref[...] + lnb_ref[...]
    y = y * jax.nn.sigmoid(y)
    o_ref[:, wmix:] = _rms(y, gn_ref[:, wmix:]).astype(BF16)


def _conv_operands(pc, pf, w3, wdw, bdw, lng, lnb, gn, tt, tile_of_step):
    wmix = pf.shape[1] // 2
    row = lambda g: (tile_of_step(g), 0)

    def halo(rows_):
        return lambda g: (jnp.maximum(tile_of_step(g) * (tt // rows_) - 1, 0), 0)

    vec = lambda a: a.reshape(1, -1)
    operands = (pc, pc, pf, pf, w3, jnp.broadcast_to(wdw[:, None, :], (CFM_K, 8, wmix)), vec(bdw),
                vec(lng), vec(lnb), vec(gn))
    specs = [pl.BlockSpec((tt, 3 * wmix), row),
             pl.BlockSpec((SHORT_HALO, 3 * wmix), halo(SHORT_HALO)),
             pl.BlockSpec((tt, 2 * wmix), row),
             pl.BlockSpec((CFM_HALO, 2 * wmix), halo(CFM_HALO)),
             _resident((SHORT_K, wmix)), _resident((CFM_K, 8, wmix)),
             _resident((1, wmix)), _resident((1, wmix)), _resident((1, wmix)),
             _resident((1, 2 * wmix))]
    scratch = [pltpu.VMEM((tt + SHORT_HALO, wmix), F32), pltpu.VMEM((8, tt + CFM_HALO, wmix), F32)]
    return operands, specs, scratch


def _cmul(ar, ai, br, bi):
    return ar * br - ai * bi, ar * bi + ai * br


def _ssm_discretise(a_re, a_im, log_dt, b_re, b_im, c_re, c_im):
    ng, ns = a_re.shape
    nh = b_re.shape[-1]
    tc = SSM_CHUNK
    dt = jnp.exp(log_dt)[:, None]
    mag = jnp.exp(a_re * dt)
    lr, li = mag * jnp.cos(a_im * dt), mag * jnp.sin(a_im * dt)
    den = a_re * a_re + a_im * a_im
    nr, ni = lr - 1.0, li
    fr = (nr * a_re + ni * a_im) / den
    fi = (ni * a_re - nr * a_im) / den
    bbr = fr[..., None] * b_re - fi[..., None] * b_im
    bbi = fr[..., None] * b_im + fi[..., None] * b_re

    pr, pi = [jnp.ones_like(lr)], [jnp.zeros_like(li)]
    for _ in range(tc):
        r, i = _cmul(pr[-1], pi[-1], lr, li)
        pr.append(r)
        pi.append(i)
    pwr, pwi = jnp.stack(pr), jnp.stack(pi)

    lbr, lbi = _cmul(pwr[:tc, :, :, None], pwi[:tc, :, :, None], bbr[None], bbi[None])
    lbt = jnp.stack([lbr, lbi]).transpose(0, 1, 2, 4, 3).reshape(2, tc, ng * nh, ns)
    ct = jnp.stack([c_re, c_im]).transpose(0, 1, 3, 2).reshape(2, ng * ns, nh)
    clr, cli = _cmul(c_re[None], c_im[None], pwr[1:, :, None, :], pwi[1:, :, None, :])
    clt = jnp.stack([clr, cli]).transpose(0, 1, 2, 4, 3).reshape(2, tc, ng * ns, nh)
    lam8 = jnp.concatenate([pwr[tc].reshape(2, 1, -1), pwi[tc].reshape(2, 1, -1)], axis=2)
    return lbt, ct, clt, lam8


def _split_bf16(x, terms):
    parts = []
    for _ in range(terms):
        part = x.astype(BF16)
        parts.append(part)
        x = x - part.astype(F32)
    return parts


def _ssm_fold_body(lbt_ref, ct_ref, clt_ref, m_ref, w_ref, g_ref):
    tc = lbt_ref.shape[1]
    rows, ns = lbt_ref.shape[2], lbt_ref.shape[3]
    nh = ct_ref.shape[2]
    gh = rows // nh
    srows = gh * ns

    def grid2(shape):
        return (lax.broadcasted_iota(jnp.int32, shape, 0), lax.broadcasted_iota(jnp.int32, shape, 1))

    r, c = grid2((ns, srows))
    spread_p = jnp.where(c % ns == r, 1.0, 0.0).astype(BF16)
    r, c = grid2((nh, rows))
    spread_h = jnp.where(c % nh == r, 1.0, 0.0).astype(BF16)
    r, c = grid2((rows, srows))
    mask_w = r // nh == c // ns
    r, c = grid2((rows, rows))
    mask_m = r // nh == c // nh
    r, c = grid2((srows, rows))
    mask_g = r // ns == c // nh

    def dot(a, b):
        return jnp.dot(a, b, preferred_element_type=F32)

    lb = [[jnp.where(mask_w, sum(dot(t, spread_p) for t in _split_bf16(lbt_ref[ri, d], 3)), 0.0)
           for d in range(tc)] for ri in range(2)]
    for tau in range(tc):
        w_ref[tau * rows:(tau + 1) * rows, :] = jnp.concatenate(
            [lb[0][tc - 1 - tau], lb[1][tc - 1 - tau]], axis=1).astype(BF16)

    c_parts = [_split_bf16(ct_ref[ri], 2) for ri in range(2)]

    def dot_f32(x, y_parts):
        xh, xl = _split_bf16(x, 2)
        return dot(xh, y_parts[0]) + dot(xh, y_parts[1]) + dot(xl, y_parts[0])

    zero = jnp.zeros((rows, rows), BF16)
    blocks = []
    for d in range(tc):
        kd = dot_f32(lb[0][d], c_parts[0]) - dot_f32(lb[1][d], c_parts[1])
        blocks.append(jnp.where(mask_m, dot(kd.astype(BF16), spread_h), 0.0).astype(BF16))
    for tau in range(tc):
        for t in range(tc):
            m_ref[tau * rows:(tau + 1) * rows, t * rows:(t + 1) * rows] = (
                blocks[t - tau] if t >= tau else zero)

    for t in range(tc):
        for ri in range(2):
            blk = jnp.where(mask_g, dot(clt_ref[ri, t].astype(BF16), spread_h), 0.0)
            g_ref[ri * srows:(ri + 1) * srows, t * rows:(t + 1) * rows] = (
                (-blk if ri else blk).astype(BF16))


def _ssm_fold(lbt, ct, clt):
    depth, _, tc, ghn, ns = lbt.shape
    nh = ct.shape[3]
    rows = ghn // 2
    srows = ct.shape[2] // 2
    out = jax.ShapeDtypeStruct((depth, 2, tc * rows, tc * rows), BF16)
    out_spec = pl.BlockSpec((None, None, tc * rows, tc * rows), lambda l, j: (l, j, 0, 0))
    assert 2 * srows == tc * rows
    return pl.pallas_call(
        _ssm_fold_body,
        grid=(depth, 2),
        in_specs=[pl.BlockSpec((None, 2, tc, rows, ns), lambda l, j: (l, 0, 0, j, 0)),
                  pl.BlockSpec((None, 2, srows, nh), lambda l, j: (l, 0, j, 0)),
                  pl.BlockSpec((None, 2, tc, srows, nh), lambda l, j: (l, 0, 0, j, 0))],
        out_specs=[out_spec] * 3,
        out_shape=[out] * 3,
        compiler_params=_cparams("parallel", "parallel"),
        name="ssm_fold",
    )(lbt, ct, clt)


def _ssm_body(u_ref, m_ref, w_ref, g_ref, a_ref, d_ref, wglu_ref, gn_ref, o_ref, state, zbuf, sbuf,
              zrow, ysc, uscr, tok, *, half, group):
    steps = u_ref.shape[0]
    width = u_ref.shape[1] // group
    ch = 2 * half
    nst = a_ref.shape[2] // 2
    assert half == LANES

    for b in range(group):
        part = u_ref[:, b * width:(b + 1) * width].astype(F32)
        for s in range(width // LANES):
            uscr[s, pl.ds(b, steps, stride=group), :] = part[:, s * LANES:(s + 1) * LANES]

    @pl.when(pl.program_id(1) == 0)
    def _():
        state[...] = jnp.zeros(state.shape, F32)

    lanes = [[slice(t * ch + j * half, t * ch + (j + 1) * half) for t in range(SSM_CHUNK)]
             for j in range(2)]
    lhs = [jnp.concatenate([uscr[s.start // LANES] for s in lanes[j]], axis=1).astype(BF16)
           for j in range(2)]
    for j in range(2):
        zbuf[j] = jnp.dot(lhs[j], w_ref[j], preferred_element_type=F32)

    lam = [(jnp.broadcast_to(a_ref[j, :, :nst], (group, nst)),
            jnp.broadcast_to(a_ref[j, :, nst:], (group, nst))) for j in range(2)]

    def chunk(r, carry):
        r0 = pl.multiple_of(r * group, group)
        out = []
        for j in range(2):
            sr, si = carry[2 * j], carry[2 * j + 1]
            sbuf[j, pl.ds(r0, group), :] = jnp.concatenate([sr, si], axis=1)
            z = zbuf[j, pl.ds(r0, group), :]
            ar, ai = lam[j]
            out += [ar * sr - ai * si + z[:, :nst], ar * si + ai * sr + z[:, nst:]]
        return tuple(out)

    final = lax.fori_loop(0, steps, chunk, tuple(state[i] for i in range(4)), unroll=8)
    for i in range(4):
        state[i] = final[i]

    for j in range(2):
        y = (jnp.dot(lhs[j], m_ref[j], preferred_element_type=F32)
             + jnp.dot(sbuf[j].astype(BF16), g_ref[j], preferred_element_type=F32))
        for t, s in enumerate(lanes[j]):
            yy = y[:, t * half:(t + 1) * half] + d_ref[:, s] * uscr[s.start // LANES]
            zrow[:, s] = jax.nn.gelu(yy).astype(BF16)

    for t in range(SSM_CHUNK):
        gl = jnp.dot(zrow[:, t * ch:(t + 1) * ch], wglu_ref[...], preferred_element_type=F32)
        _to_slabs(ysc, _rms(gl[:, :ch] * jax.nn.sigmoid(gl[:, ch:]), gn_ref[...]))
        for b in range(group):
            for s in range(ysc.shape[0]):
                tok[b, s, pl.ds(t, steps, stride=SSM_CHUNK), :] = (
                    ysc[s, pl.ds(b, steps, stride=group), :])
    for b in range(group):
        o_ref[b] = jnp.concatenate([tok[b, s] for s in range(tok.shape[1])], axis=1).astype(BF16)


def _ssm_mixer(u_rows, mats, d_skip, wglu, gn, seq, tile_rows):
    n_groups = u_rows.shape[0]
    width = SSM_CHUNK * wglu.shape[0]
    ch = width // SSM_CHUNK
    m, w, g, a = mats
    group = u_rows.shape[2] // width
    rows_ = tile_rows * group
    d_row = jnp.tile(d_skip, SSM_CHUNK).reshape(1, width)
    nst = a.shape[2] // 2
    out = pl.pallas_call(
        functools.partial(_ssm_body, half=ch // 2, group=group),
        grid=(n_groups, seq // SSM_CHUNK // tile_rows),
        in_specs=[pl.BlockSpec((None, tile_rows, group * width), lambda b, t: (b, t, 0)),
                  _resident(m.shape), _resident(w.shape), _resident(g.shape),
                  _resident(a.shape), _resident((1, width)), _resident(wglu.shape),
                  _resident((1, ch))],
        out_specs=pl.BlockSpec((group, tile_rows * SSM_CHUNK, ch), lambda b, t: (b, t, 0)),
        out_shape=jax.ShapeDtypeStruct((n_groups * group, seq, ch), BF16),
        scratch_shapes=[pltpu.VMEM((4, group, nst), F32),
                        pltpu.VMEM((2, rows_, 2 * nst), F32), pltpu.VMEM((2, rows_, 2 * nst), F32),
                        pltpu.VMEM((rows_, width), BF16),
                        pltpu.VMEM((ch // LANES, rows_, LANES), F32),
                        pltpu.VMEM((width // LANES, rows_, LANES), F32),
                        pltpu.VMEM((group, ch // LANES, tile_rows * SSM_CHUNK, LANES), F32)],
        compiler_params=_cparams("parallel", "arbitrary"),
        name="ssm_mixer",
    )(u_rows, m, w, g, a, d_row, wglu, gn.reshape(1, ch))
    return out.reshape(n_groups * group * seq, ch)


def _attn_body(q_ref, k_ref, v_ref, o_ref, st_ref, sbuf, pbuf, *, n_heads, qb):
    blk = ATT_BLK
    wd = q_ref.shape[1]
    nblk = q_ref.shape[0] // blk

    qi = lax.broadcasted_iota(jnp.int32, (blk, 2 * blk), 0)
    kj = lax.broadcasted_iota(jnp.int32, (blk, 2 * blk), 1)
    rel = blk + qi - kj
    bias_any = jnp.where((rel >= 0) & (rel <= blk), 0.0, NEG_BIG)
    bias_first = jnp.where(kj <= qi, 0.0, NEG_BIG)
    head_of_lane = lax.broadcasted_iota(jnp.int32, (1, wd), 1) // HEAD_DIM
    st_lane = lax.broadcasted_iota(jnp.int32, (1, st_ref.shape[1]), 1)

    def rows_of(f):
        return f * blk if isinstance(f, int) else pl.multiple_of(f * blk, blk)

    def window_of(f):
        first = (f % qb) == 0
        if isinstance(f, int):
            return (f if first else f - 1) * blk, first
        return pl.multiple_of((f - jnp.where(first, 0, 1)) * blk, blk), first

    def scores(f, slot):
        q = q_ref[pl.ds(rows_of(f), blk), :]
        qs = jnp.concatenate([jnp.where(head_of_lane == h, q, jnp.zeros_like(q))
                              for h in range(n_heads)], axis=0)
        sbuf[slot] = lax.dot_general(qs, k_ref[pl.ds(window_of(f)[0], 2 * blk), :],
                                     (((1,), (1,)), ((), ())), preferred_element_type=F32)

    def softmax(f, slot):
        first = window_of(f)[1]
        bias = (bias_first if first else bias_any) if isinstance(f, int) else jnp.where(
            first, bias_first, bias_any)
        stats = jnp.zeros((blk, st_ref.shape[1]), F32)
        for h in range(n_heads):
            s = sbuf[slot, h * blk:(h + 1) * blk, :] + bias
            m = jnp.max(s, axis=-1, keepdims=True)
            p = jnp.exp(s - m)
            pbuf[slot, h * blk:(h + 1) * blk, :] = p.astype(BF16)
            stats = jnp.where(st_lane == h, m, stats)
            stats = jnp.where(st_lane == n_heads + h, jnp.sum(p, axis=-1, keepdims=True), stats)
        st_ref[pl.ds(rows_of(f), blk), :] = stats

    def values(f, slot):
        o_all = jnp.dot(pbuf[slot], v_ref[pl.ds(window_of(f)[0], 2 * blk), :],
                        preferred_element_type=F32)
        o = o_all[0:blk]
        for h in range(1, n_heads):
            o = jnp.where(head_of_lane == h, o_all[h * blk:(h + 1) * blk], o)
        o_ref[pl.ds(rows_of(f), blk), :] = o.astype(BF16)

    def step(f, slot):
        values(f - 2, slot)
        scores(f, slot)
        softmax(f - 1, 1 - slot)

    scores(0, 0)
    scores(1, 1)
    softmax(0, 0)

    def pair(it, carry):
        step(2 * it + 2, 0)
        step(2 * it + 3, 1)
        return carry

    lax.fori_loop(0, (nblk - 2) // 2, pair, 0)
    softmax(nblk - 1, 1)
    values(nblk - 2, 0)
    values(nblk - 1, 1)


ST_LANES = LANES


def _attention(q, k, v):
    batch, dilation, rows_, wd = q.shape
    seq = dilation * rows_
    n_heads = wd // HEAD_DIM
    qb = rows_ // ATT_BLK
    assert qb >= 2 and qb & (qb - 1) == 0
    flat = lambda a: a.reshape(batch, seq, wd)
    spec = pl.BlockSpec((None, seq, wd), lambda b: (b, 0, 0))
    o, st = pl.pallas_call(
        functools.partial(_attn_body, n_heads=n_heads, qb=qb),
        grid=(batch,),
        in_specs=[spec, spec, spec],
        out_specs=[spec, pl.BlockSpec((None, seq, ST_LANES), lambda b: (b, 0, 0))],
        out_shape=[jax.ShapeDtypeStruct((batch, seq, wd), BF16),
                   jax.ShapeDtypeStruct((batch, seq, ST_LANES), F32)],
        scratch_shapes=[pltpu.VMEM((2, n_heads * ATT_BLK, 2 * ATT_BLK), F32),
                        pltpu.VMEM((2, n_heads * ATT_BLK, 2 * ATT_BLK), BF16)],
        compiler_params=_cparams("parallel"),
        name=f"attn_d{dilation}",
    )(flat(q), flat(k), flat(v))
    return (o.reshape(batch, dilation, rows_, wd), st.reshape(batch, dilation, rows_, ST_LANES))


def _token_order(block, d, bufs):
    slabs, tm, _ = bufs[0].shape
    stage = 0
    while d > 1:
        f = 4 if d % 4 == 0 else d
        nxt, rows, dst = d // f, tm // d, bufs[stage % 2]
        for r1 in range(nxt):
            for a in range(f):
                blk = block(r1 + nxt * a)
                for s in range(slabs):
                    dst[s, pl.ds(r1 * (tm // nxt) + a, rows, stride=f), :] = (
                        blk[:, s * LANES:(s + 1) * LANES])

        def block(r, dst=dst, size=tm // nxt):
            return jnp.concatenate([dst[s, pl.ds(r * size, size), :] for s in range(slabs)], axis=1)

        d, stage = nxt, stage + 1
    return block(0)


def _att_mix(o_refs, s_refs, gn_ref, oscr, sscr, dilations):
    tm, wd = oscr.shape[3], oscr.shape[2] * LANES
    n_heads = wd // HEAD_DIM
    outs, lses = [], []
    for p, d in enumerate(dilations):
        outs.append(_token_order(lambda r, p=p: o_refs[p][r].astype(F32), d,
                                 (oscr.at[p, 0], oscr.at[p, 1])))
        lses.append(_token_order(lambda r, p=p: s_refs[p][r], d, (sscr.at[p, 0], sscr.at[p, 1])))
    m = functools.reduce(jnp.maximum, lses)
    head_of_lane = lax.broadcasted_iota(jnp.int32, (1, wd), 1) // HEAD_DIM

    def widen(x, lane0):
        wide = jnp.zeros((tm, wd), F32)
        for h in range(n_heads):
            wide = jnp.where(head_of_lane == h, x[:, lane0 + h:lane0 + h + 1], wide)
        return wide

    num = jnp.zeros((tm, wd), F32)
    den = jnp.zeros((tm, wd), F32)
    for st, o in zip(lses, outs):
        e = widen(jnp.exp(st - m), 0)
        num = num + e * o
        den = den + e * widen(st, n_heads)
    return _rms(num / den, gn_ref[...]).astype(BF16)


FF_CHUNK = 256


def _out_ffn_body(x_ref, ys_ref, *rest, final_norm, dilations, tiles, per_seq):
    npat = len(dilations)
    conv_in, rest = rest[:10], rest[10:]
    o_refs, s_refs = rest[:npat], rest[npat:2 * npat]
    (gna_ref, wo_ref, gf_ref, wg_ref, wu_ref, wd_ref, gl_ref, o_ref, act, ya, oscr, sscr, ycc, zb3,
     zs, hbuf) = rest[2 * npat:]
    g = pl.program_id(0)
    first_of_seq = jnp.minimum(g, tiles - 1) % per_seq == 0

    @pl.when(g == 0)
    def _():
        ya[1] = jnp.zeros(ya.shape[1:], BF16)
        ycc[1] = jnp.zeros(ycc.shape[1:], BF16)

    cur, prev = g % 2, (g + 1) % 2
    mixed = jnp.concatenate([ycc[prev], ys_ref[...], ya[prev]], axis=1)
    x1 = x_ref[...] + jnp.dot(mixed, wo_ref[...], preferred_element_type=F32)
    o_ref[...] = x1
    hbuf[...] = _rms(x1, gf_ref[...]).astype(BF16)
    dff = wg_ref.shape[1]
    for c in range(dff // FF_CHUNK):
        sl = slice(c * FF_CHUNK, (c + 1) * FF_CHUNK)
        gate = jnp.dot(hbuf[...], wg_ref[:, sl], preferred_element_type=F32)
        up = jnp.dot(hbuf[...], wu_ref[:, sl], preferred_element_type=F32)
        act[:, sl] = (gate * jax.nn.sigmoid(gate) * up).astype(BF16)
    x2 = o_ref[...] + jnp.dot(act[...], wd_ref[...], preferred_element_type=F32)
    if final_norm:
        x2 = _rms(x2, gl_ref[...])
    o_ref[...] = x2

    ya[cur] = _att_mix(o_refs, s_refs, gna_ref, oscr, sscr, dilations)
    _conv_tile(*conv_in, ycc.at[cur], zb3, zs, jnp.where(first_of_seq, 0.0, 1.0))


def _out_ffn(x, conv_args, ys, att_outs, att_stats, gna, wo, gf, wg, wu, wd, gl, final_norm, tm):
    n, d = x.shape
    dff = wg.shape[1]
    batch, _, _, wa = att_outs[0].shape
    seq = n // batch
    dilations = tuple(o.shape[1] for o in att_outs)
    per_seq = seq // tm
    tiles = n // tm
    row = lambda g: (jnp.maximum(g - 1, 0), 0)
    ahead = lambda g: jnp.minimum(g, tiles - 1)
    conv_ops, conv_specs, conv_scratch = _conv_operands(*conv_args, tm, ahead)
    wcc = conv_ops[-1].shape[1]

    def res_spec(dl, width):
        return pl.BlockSpec((None, dl, tm // dl, width),
                            lambda g: (ahead(g) // per_seq, 0, ahead(g) % per_seq, 0))

    return pl.pallas_call(
        functools.partial(_out_ffn_body, final_norm=final_norm, dilations=dilations, tiles=tiles,
                          per_seq=per_seq),
        grid=(tiles + 1,),
        in_specs=[pl.BlockSpec((tm, d), row), pl.BlockSpec((tm, ys.shape[1]), row)] + conv_specs
        + [res_spec(dl, wa) for dl in dilations] + [res_spec(dl, ST_LANES) for dl in dilations]
        + [_resident((1, wa)), _resident(wo.shape), _resident((1, d)), _resident(wg.shape),
           _resident(wu.shape), _resident(wd.shape), _resident((1, d))],
        out_specs=pl.BlockSpec((tm, d), row),
        out_shape=jax.ShapeDtypeStruct((n, d), F32),
        scratch_shapes=[pltpu.VMEM((tm, dff), BF16), pltpu.VMEM((2, tm, wa), BF16),
                        pltpu.VMEM((len(dilations), 2, wa // LANES, tm, LANES), F32),
                        pltpu.VMEM((len(dilations), 2, ST_LANES // LANES, tm, LANES), F32),
                        pltpu.VMEM((2, tm, wcc), BF16)] + conv_scratch
        + [pltpu.VMEM((tm, d), BF16)],
        compiler_params=_cparams("arbitrary"),
        name="out_ffn",
    )(x, ys, *conv_ops, *att_outs, *att_stats, gna.reshape(1, wa), wo, gf.reshape(1, d), wg, wu,
      wd, gl.reshape(1, d))


def _rope_tables(seq, wd):
    half = HEAD_DIM // 2
    inv = ROPE_THETA ** (-jnp.arange(0, HEAD_DIM, 2, dtype=F32) / HEAD_DIM)
    ang = jnp.arange(seq, dtype=F32)[:, None] * inv[None, :]
    cos, sin = jnp.cos(ang), jnp.sin(ang)
    reps = wd // HEAD_DIM
    cos_t = jnp.tile(jnp.concatenate([cos, cos], axis=1), (1, reps))
    sin_t = jnp.tile(jnp.concatenate([-sin, sin], axis=1), (1, reps))
    scale = HEAD_DIM ** -0.5
    return cos_t * scale, sin_t * scale, cos_t, sin_t


def kernel(x, norm_mix_g, w_in, conv3_w, cfm_dw_w, cfm_dw_b, cfm_ln_g, cfm_ln_b, s5_a_re, s5_a_im, s5_log_dt, s5_b_re, s5_b_im, s5_c_re, s5_c_im, s5_d, s5_glu_w, grp_norm_g, w_out, norm_ffn_g, w_gate, w_up, w_down, final_norm_g):
    batch, seq, d_model = x.shape
    depth = w_in.shape[0]
    wmix = w_in.shape[2] // 9
    n = batch * seq
    assert wmix == 2 * 8 * SSM_CH and all(w // d == ATT_BLK for w, d in DILATED_CFG)
    assert seq % (DILATED_CFG[-1][1] * ATT_BLK) == 0 and seq % 512 == 0
    tm = 512
    group = math.gcd(batch, 8)
    rope_tabs = _rope_tables(seq, wmix)

    dilations = tuple(dil for _, dil in DILATED_CFG)
    npat = len(dilations)

    w_in, s5_glu_w, w_out, w_gate, w_up, w_down = (
        a.astype(BF16) for a in (w_in, s5_glu_w, w_out, w_gate, w_up, w_down))
    lbt, ct, clt, lam8 = jax.vmap(_ssm_discretise)(s5_a_re, s5_a_im, s5_log_dt, s5_b_re, s5_b_im,
                                                   s5_c_re, s5_c_im)
    ssm_mats = (*_ssm_fold(lbt, ct, clt), lam8)

    xf = x.reshape(n, d_model)
    for layer in range(depth):
        pc, pf, u, *qkv = _in_proj(xf, norm_mix_g[layer], w_in[layer], rope_tabs,
                                   batch, seq, 2 * tm, dilations, group)
        gn = grp_norm_g[layer]
        conv_args = (pc, pf, conv3_w[layer], cfm_dw_w[layer], cfm_dw_b[layer], cfm_ln_g[layer],
                     cfm_ln_b[layer], gn[:2 * wmix])
        ys = _ssm_mixer(u, [a[layer] for a in ssm_mats], s5_d[layer], s5_glu_w[layer],
                        gn[2 * wmix:3 * wmix], seq, tm // SSM_CHUNK)
        outs, stats = zip(*[_attention(qkv[p], qkv[npat + p], qkv[2 * npat + p])
                            for p in range(npat)])
        xf = _out_ffn(xf, conv_args, ys, outs, stats, gn[3 * wmix:], w_out[layer],
                      norm_ffn_g[layer],
                      w_gate[layer], w_up[layer], w_down[layer], final_norm_g,
                      layer == depth - 1, tm)
    return xf.reshape(batch, seq, d_model)
```
